```python
import math
import jax, jax.numpy as jnp
from jax import lax
import numpy as np

D_MODEL = 1024
BATCH = 8
SEQ = 2048
DEPTH = 2
DEC_BATCH = 128
DEC_SEQ = 8
PAST_LEN = 16384
PAGE_SIZE = 128

D_A = D_MODEL // 2
H_A = 4
HD_A = D_A // H_A
CHUNK = 128
D_B = D_MODEL // 2
H_B = 8
BH_B = D_B // H_B
CONV_W = 4
LRU_C = 8.0
D_C = D_MODEL // 2
HG_C = 16
G_C = D_C // HG_C
P_STATE = 64
N_BRANCH = 3
D_BR = D_MODEL // 2
IN_COLS = 2 * D_A + 2 * D_B + D_C + N_BRANCH * D_MODEL
D_FF = 11 * D_MODEL // 4
N_EXPERTS = 8
TOP_K = 2
D_EXP = D_FF // 2
N_DENSE = (DEPTH + 1) // 2
N_MOE = DEPTH // 2
EPS = 1e-6

kernel_name = 'hybrid_gmlp_rglru_s5_decoder_step'

F32 = jnp.float32


def _rmsnorm(x, g):
    x32 = x.astype(F32)
    y = x32 * lax.rsqrt(jnp.mean(x32 * x32, axis=-1, keepdims=True) + EPS)
    return (y * g.astype(F32)).astype(x.dtype)


def _layernorm(x, g, b):
    x32 = x.astype(F32)
    xc = x32 - jnp.mean(x32, axis=-1, keepdims=True)
    var = jnp.mean(xc * xc, axis=-1, keepdims=True)
    return (xc * lax.rsqrt(var + EPS) * g.astype(F32) + b.astype(F32)).astype(x.dtype)


def _linear_combine(left, right):
    a1, b1 = left
    a2, b2 = right
    return a1 * a2, a2 * b1 + b2


def _chunk_spatial_gate(v, ws, bs):
    n, t, _ = v.shape
    t_pad = -(-t // CHUNK) * CHUNK
    vp = jnp.pad(v, ((0, 0), (0, t_pad - t), (0, 0))).reshape(n, t_pad // CHUNK, CHUNK, H_A, HD_A)
    causal = jnp.tril(jnp.ones((CHUNK, CHUNK), dtype=bool))
    wm = jnp.where(causal[None], ws, jnp.zeros_like(ws))
    mixed = jnp.einsum('hts,ncshd->ncthd', wm, vp) + bs.T[:, :, None]
    return mixed.reshape(n, t_pad, D_A)[:, :t]


def _causal_dwconv(x, buf, w, b):
    t = x.shape[1]
    xp = jnp.concatenate([buf.astype(x.dtype), x], axis=1)
    y = b + xp[:, 0:t] * w[0]
    for k in range(1, CONV_W):
        y = y + xp[:, k:k + t] * w[k]
    return y, xp[:, t:]


def _rg_lru(xc, h0, wa, ba, wx, bx, lam):
    n, t, _ = xc.shape
    xh = xc.reshape(n, t, H_B, BH_B)
    r = jax.nn.sigmoid(jnp.einsum('nthi,hij->nthj', xh, wa).reshape(n, t, D_B) + ba).astype(F32)
    i = jax.nn.sigmoid(jnp.einsum('nthi,hij->nthj', xh, wx).reshape(n, t, D_B) + bx)
    log_a = -LRU_C * r * jax.nn.softplus(-lam.astype(F32))
    a = jnp.exp(log_a)
    drive = jnp.sqrt(-jnp.expm1(2.0 * log_a)) * (i * xc).astype(F32)
    drive = drive.at[:, 0].add(a[:, 0] * h0.astype(F32))
    _, hs = lax.associative_scan(_linear_combine, (a, drive), axis=1)
    return hs, hs[:, -1]


def _s5(u, h0_re, h0_im, lam_re, lam_im, log_dt, b_re, b_im, c_re, c_im, d):
    n, t, _ = u.shape
    lam = lax.complex(lam_re.astype(F32), lam_im.astype(F32))
    dt = jnp.exp(log_dt.astype(F32))[:, None]
    a_bar = jnp.exp(lam * dt)
    b_bar = ((a_bar - 1.0) / lam)[:, :, None] * lax.complex(b_re.astype(F32), b_im.astype(F32))
    c_mat = lax.complex(c_re.astype(F32), c_im.astype(F32))
    ug = u.astype(F32).reshape(n, t, G_C, HG_C).astype(jnp.complex64)
    bu = jnp.einsum('gph,ntgh->ntgp', b_bar, ug)
    bu = bu.at[:, 0].add(a_bar * lax.complex(h0_re.astype(F32), h0_im.astype(F32)))
    a_seq = jnp.broadcast_to(a_bar, bu.shape)
    _, xs = lax.associative_scan(_linear_combine, (a_seq, bu), axis=1)
    y = jnp.einsum('ghp,ntgp->ntgh', c_mat, xs).real.reshape(n, t, D_C)
    y = y + d.astype(F32) * u.astype(F32)
    last = xs[:, -1]
    return y.astype(u.dtype), jnp.real(last), jnp.imag(last)


def _mixer(h, conv_buf, lru_h, s5_re, s5_im, w_in, b_gate, a_ln_g, a_ln_b, a_ws, a_bs,
           b_conv_w, b_conv_b, b_wa, b_ba, b_wx, b_bx, b_lam, c_lam_re, c_lam_im, c_log_dt,
           c_b_re, c_b_im, c_c_re, c_c_im, c_d, c_glu_w, c_glu_b, w_branch, w_out):
    n, t, _ = h.shape
    proj = h @ w_in
    o1 = D_A
    o2 = o1 + D_A
    o3 = o2 + D_B
    o4 = o3 + D_B
    o5 = o4 + D_C
    ua, va, xb, gb, uc, gl = jnp.split(proj, [o1, o2, o3, o4, o5], axis=-1)
    va = _layernorm(jax.nn.gelu(va), a_ln_g, a_ln_b)
    ya = jax.nn.gelu(ua) * _chunk_spatial_gate(va, a_ws, a_bs)
    xc, conv_new = _causal_dwconv(xb, conv_buf, b_conv_w, b_conv_b)
    hs, h_last = _rg_lru(xc, lru_h, b_wa, b_ba, b_wx, b_bx, b_lam)
    yb = hs.astype(h.dtype) * jax.nn.gelu(gb)
    ys, s_re, s_im = _s5(uc, s5_re, s5_im, c_lam_re, c_lam_im, c_log_dt, c_b_re, c_b_im,
                         c_c_re, c_c_im, c_d)
    z = jax.nn.gelu(ys)
    yc = z * jax.nn.sigmoid(z @ c_glu_w + c_glu_b)
    br = jnp.stack([ya, yb, yc], axis=2)
    p = jnp.einsum('ntkc,kcd->ntkd', br, w_branch)
    g = jax.nn.sigmoid(gl + b_gate).reshape(n, t, N_BRANCH, D_MODEL)
    out = jnp.sum(g * p, axis=2) @ w_out
    return out, va, conv_new, h_last, s_re, s_im


def _swiglu(h, wg, wu, wd):
    return (jax.nn.silu(h @ wg) * (h @ wu)) @ wd


def _moe(h, rw, rb, wg, wu, wd):
    logits = (h @ rw + rb).astype(F32)
    top_v, top_i = lax.top_k(logits, TOP_K)
    probs = jax.nn.softmax(top_v, axis=-1)
    comb = jnp.sum(jax.nn.one_hot(top_i, N_EXPERTS, dtype=F32) * probs[..., None], axis=-2)
    comb = comb.astype(h.dtype)
    out = jnp.zeros_like(h)
    for e in range(N_EXPERTS):
        out = out + comb[..., e:e + 1] * _swiglu(h, wg[e], wu[e], wd[e])
    return out


def _trunk(x, conv_s, lru_s, s5re_s, s5im_s, mix_w, norm_mix_g, norm_ffn_g, ffn_w_gate,
           ffn_w_up, ffn_w_down, moe_router_w, moe_router_b, moe_w_gate, moe_w_up,
           moe_w_down, final_norm_g):
    vas, convs, lrus, sres, sims = [], [], [], [], []
    for l in range(DEPTH):
        hn = _rmsnorm(x, norm_mix_g[l])
        mix, va, cs, hl, sr, si = _mixer(hn, conv_s[l], lru_s[l], s5re_s[l], s5im_s[l],
                                         *[w[l] for w in mix_w])
        x = x + mix
        hn = _rmsnorm(x, norm_ffn_g[l])
        j = l // 2
        if l % 2 == 0:
            x = x + _swiglu(hn, ffn_w_gate[j], ffn_w_up[j], ffn_w_down[j])
        else:
            x = x + _moe(hn, moe_router_w[j], moe_router_b[j], moe_w_gate[j], moe_w_up[j],
                         moe_w_down[j])
        vas.append(va)
        convs.append(cs)
        lrus.append(hl)
        sres.append(sr)
        sims.append(si)
    return (_rmsnorm(x, final_norm_g), jnp.stack(vas), jnp.stack(convs), jnp.stack(lrus),
            jnp.stack(sres), jnp.stack(sims))


def setup_inputs(seed: int = 0) -> dict:
    key = jax.random.key(seed)
    ks = jax.random.split(key, 42)

    def nrm(i, shape, scale):
        return scale * jax.random.normal(ks[i], shape, F32)

    u_lam = jax.random.uniform(ks[19], (DEPTH, D_B), F32, 0.9, 0.999)
    base = u_lam ** (1.0 / LRU_C)
    lam_im = math.pi * jnp.arange(P_STATE, dtype=F32)
    return {
        'x_prompt': nrm(0, (BATCH, SEQ, D_MODEL), 1.0),
        'x_sample': nrm(1, (DEC_BATCH, DEC_SEQ, D_MODEL), 1.0),
        'state_conv_b': nrm(2, (DEPTH, DEC_BATCH, CONV_W - 1, D_B), 1.0),
        'state_lru_b': nrm(3, (DEPTH, DEC_BATCH, D_B), 0.5),
        'state_s5_re': nrm(4, (DEPTH, DEC_BATCH, G_C, P_STATE), 0.1),
        'state_s5_im': nrm(5, (DEPTH, DEC_BATCH, G_C, P_STATE), 0.1),
        'norm_mix_g': 1.0 + nrm(6, (DEPTH, D_MODEL), 0.02),
        'w_in': nrm(7, (DEPTH, D_MODEL, IN_COLS), D_MODEL ** -0.5),
        'b_gate': nrm(8, (DEPTH, N_BRANCH * D_MODEL), 0.02),
        'a_ln_g': 1.0 + nrm(9, (DEPTH, D_A), 0.02),
        'a_ln_b': nrm(10, (DEPTH, D_A), 0.02),
        'a_ws': nrm(11, (DEPTH, H_A, CHUNK, CHUNK), CHUNK ** -0.5),
        'a_bs': 1.0 + nrm(12, (DEPTH, H_A, CHUNK), 0.02),
        'b_conv_w': nrm(13, (DEPTH, CONV_W, D_B), CONV_W ** -0.5),
        'b_conv_b': nrm(14, (DEPTH, D_B), 0.01),
        'b_wa': nrm(15, (DEPTH, H_B, BH_B, BH_B), BH_B ** -0.5),
        'b_ba': nrm(16, (DEPTH, D_B), 0.1),
        'b_wx': nrm(17, (DEPTH, H_B, BH_B, BH_B), BH_B ** -0.5),
        'b_bx': nrm(18, (DEPTH, D_B), 0.1),
        'b_lam': jnp.log(base) - jnp.log1p(-base),
        'c_lam_re': -0.5 + nrm(20, (DEPTH, G_C, P_STATE), 0.01),
        'c_lam_im': lam_im + nrm(21, (DEPTH, G_C, P_STATE), 0.01),
        'c_log_dt': jax.random.uniform(ks[22], (DEPTH, G_C), F32, math.log(1e-3), math.log(1e-1)),
        'c_b_re': nrm(23, (DEPTH, G_C, P_STATE, HG_C), (2.0 * HG_C) ** -0.5),
        'c_b_im': nrm(24, (DEPTH, G_C, P_STATE, HG_C), (2.0 * HG_C) ** -0.5),
        'c_c_re': nrm(25, (DEPTH, G_C, HG_C, P_STATE), (2.0 * P_STATE) ** -0.5),
        'c_c_im': nrm(26, (DEPTH, G_C, HG_C, P_STATE), (2.0 * P_STATE) ** -0.5),
        'c_d': nrm(27, (DEPTH, D_C), 1.0),
        'c_glu_w': nrm(28, (DEPTH, D_C, D_C), D_C ** -0.5),
        'c_glu_b': nrm(29, (DEPTH, D_C), 0.02),
        'w_branch': nrm(30, (DEPTH, N_BRANCH, D_BR, D_MODEL), D_BR ** -0.5),
        'w_out': nrm(31, (DEPTH, D_MODEL, D_MODEL), D_MODEL ** -0.5),
        'norm_ffn_g': 1.0 + nrm(32, (DEPTH, D_MODEL), 0.02),
        'ffn_w_gate': nrm(33, (N_DENSE, D_MODEL, D_FF), D_MODEL ** -0.5),
        'ffn_w_up': nrm(34, (N_DENSE, D_MODEL, D_FF), D_MODEL ** -0.5),
        'ffn_w_down': nrm(35, (N_DENSE, D_FF, D_MODEL), D_FF ** -0.5),
        'moe_router_w': nrm(36, (N_MOE, D_MODEL, N_EXPERTS), D_MODEL ** -0.5),
        'moe_router_b': nrm(37, (N_MOE, N_EXPERTS), 0.01),
        'moe_w_gate': nrm(38, (N_MOE, N_EXPERTS, D_MODEL, D_EXP), D_MODEL ** -0.5),
        'moe_w_up': nrm(39, (N_MOE, N_EXPERTS, D_MODEL, D_EXP), D_MODEL ** -0.5),
        'moe_w_down': nrm(40, (N_MOE, N_EXPERTS, D_EXP, D_MODEL), D_EXP ** -0.5),
        'final_norm_g': 1.0 + nrm(41, (D_MODEL,), 0.02),
    }


def reference(x_prompt, x_sample, state_conv_b, state_lru_b, state_s5_re, state_s5_im,
              norm_mix_g, w_in, b_gate, a_ln_g, a_ln_b, a_ws, a_bs, b_conv_w, b_conv_b,
              b_wa, b_ba, b_wx, b_bx, b_lam, c_lam_re, c_lam_im, c_log_dt, c_b_re, c_b_im,
              c_c_re, c_c_im, c_d, c_glu_w, c_glu_b, w_branch, w_out, norm_ffn_g,
              ffn_w_gate, ffn_w_up, ffn_w_down, moe_router_w, moe_router_b, moe_w_gate,
              moe_w_up, moe_w_down, final_norm_g):
    mix_w = (w_in, b_gate, a_ln_g, a_ln_b, a_ws, a_bs, b_conv_w, b_conv_b, b_wa, b_ba, b_wx,
             b_bx, b_lam, c_lam_re, c_lam_im, c_log_dt, c_b_re, c_b_im, c_c_re, c_c_im, c_d,
             c_glu_w, c_glu_b, w_branch, w_out)
    nb = x_prompt.shape[0]
    conv0 = jnp.zeros((DEPTH, nb, CONV_W - 1, D_B), x_prompt.dtype)
    lru0 = jnp.zeros((DEPTH, nb, D_B), F32)
    s50 = jnp.zeros((DEPTH, nb, G_C, P_STATE), F32)
    y_prompt, _, conv_b_p, lru_b_p, s5_re_p, s5_im_p = _trunk(
        x_prompt, conv0, lru0, s50, s50, mix_w, norm_mix_g, norm_ffn_g, ffn_w_gate, ffn_w_up,
        ffn_w_down, moe_router_w, moe_router_b, moe_w_gate, moe_w_up, moe_w_down, final_norm_g)
    y_sample, v_a_s, conv_b_s, lru_b_s, s5_re_s, s5_im_s = _trunk(
        x_sample, state_conv_b, state_lru_b, state_s5_re, state_s5_im, mix_w, norm_mix_g,
        norm_ffn_g, ffn_w_gate, ffn_w_up, ffn_w_down, moe_router_w, moe_router_b, moe_w_gate,
        moe_w_up, moe_w_down, final_norm_g)
    return (y_prompt, y_sample, conv_b_p, lru_b_p, s5_re_p, s5_im_p,
            v_a_s, conv_b_s, lru_b_s, s5_re_s, s5_im_s)
```

```python
import functools
import math

import jax
import jax.numpy as jnp
from jax import lax
from jax.experimental import pallas as pl
from jax.experimental.pallas import tpu as pltpu

D_MODEL = 1024
DEPTH = 2
CHUNK = 128
D_A = D_MODEL // 2
H_A = 4
HD_A = D_A // H_A
D_B = D_MODEL // 2
H_B = 8
BH_B = D_B // H_B
CONV_W = 4
LRU_C = 8.0
D_C = D_MODEL // 2
HG_C = 16
G_C = D_C // HG_C
P_STATE = 64
N_BRANCH = 3
D_BR = D_MODEL // 2
IN_COLS = 2 * D_A + 2 * D_B + D_C + N_BRANCH * D_MODEL
D_FF = 11 * D_MODEL // 4
N_EXPERTS = 8
TOP_K = 2
D_EXP = D_FF // 2
EPS = 1e-6

F32 = jnp.float32
BF16 = jnp.bfloat16

COL_BLK = 512
N_COL_BLK = IN_COLS // COL_BLK
BLK_UA, BLK_VA, BLK_XB, BLK_GB, BLK_UC, BLK_GATE = 0, 1, 2, 3, 4, 5
S5_HALVES = 2
S5_HALF_IN = D_C // S5_HALVES
S5_HALF_RI = (G_C // S5_HALVES) * P_STATE
S5_HALF_ST = 2 * S5_HALF_RI
S5_LANES = S5_HALVES * S5_HALF_ST
LANE_TILE = 128
SUBLANE_TILE = 8
VMEM_LIMIT = 56 * 1024 * 1024


def _resident(shape):
    zeros = (0,) * len(shape)
    return pl.BlockSpec(shape, lambda *_: zeros)


def _params(sem):
    return pltpu.CompilerParams(dimension_semantics=sem, vmem_limit_bytes=VMEM_LIMIT)


def _rms(x, g):
    return x * lax.rsqrt(jnp.mean(x * x, axis=-1, keepdims=True) + EPS) * g


def _bdot(a, b):
    return jnp.dot(a.astype(BF16), b, preferred_element_type=F32)


def _in_proj_kernel(x_ref, g_ref, w_ref, bg_ref, lng_ref, lnb_ref, o_ref, hn_ref):
    j = pl.program_id(1)

    @pl.when(j == 0)
    def _():
        hn_ref[...] = _rms(x_ref[...], g_ref[...]).astype(BF16)

    acc = jnp.dot(hn_ref[...], w_ref[...], preferred_element_type=F32)

    @pl.when((j == BLK_UA) | (j == BLK_GB))
    def _():
        o_ref[...] = jax.nn.gelu(acc)

    @pl.when(j == BLK_VA)
    def _():
        v = jax.nn.gelu(acc)
        vc = v - jnp.mean(v, axis=-1, keepdims=True)
        var = jnp.mean(vc * vc, axis=-1, keepdims=True)
        o_ref[...] = vc * lax.rsqrt(var + EPS) * lng_ref[...] + lnb_ref[...]

    @pl.when((j == BLK_XB) | (j == BLK_UC))
    def _():
        o_ref[...] = acc

    @pl.when(j >= BLK_GATE)
    def _():
        o_ref[...] = jax.nn.sigmoid(acc + bg_ref[...])


def _in_proj(x, norm_g, w_in, b_gate, ln_g, ln_b):
    r = x.shape[0]
    tm = min(r, 1024)
    return pl.pallas_call(
        _in_proj_kernel,
        grid=(r // tm, N_COL_BLK),
        in_specs=[
            pl.BlockSpec((tm, D_MODEL), lambda i, j: (i, 0)),
            _resident((1, D_MODEL)),
            pl.BlockSpec((D_MODEL, COL_BLK), lambda i, j: (0, j)),
            pl.BlockSpec((1, COL_BLK), lambda i, j: (0, jnp.maximum(j - BLK_GATE, 0))),
            _resident((1, D_A)),
            _resident((1, D_A)),
        ],
        out_specs=pl.BlockSpec((tm, COL_BLK), lambda i, j: (i, j)),
        out_shape=jax.ShapeDtypeStruct((r, IN_COLS), F32),
        scratch_shapes=[pltpu.VMEM((tm, D_MODEL), BF16)],
        compiler_params=_params(("parallel", "arbitrary")),
        name="in_proj",
    )(x, norm_g, w_in, b_gate, ln_g, ln_b)


def _mixer_a_mxu_kernel(u_ref, v_ref, wm_ref, bm_ref, o_ref, *, n_chunks):
    for c in range(n_chunks):
        rows = slice(c * CHUNK, (c + 1) * CHUNK)
        for h in range(H_A):
            cols = slice(h * HD_A, (h + 1) * HD_A)
            mixed = _bdot(wm_ref[h], v_ref[rows, cols].astype(BF16)) + bm_ref[h]
            o_ref[rows, cols] = u_ref[rows, cols] * mixed


def _mixer_a_prompt(proj, t, n, wm, bm):
    n_chunks = 4
    rows = n_chunks * CHUNK
    proj2 = proj.reshape(t, n * IN_COLS)
    out = pl.pallas_call(
        functools.partial(_mixer_a_mxu_kernel, n_chunks=n_chunks),
        grid=(t // rows, n),
        in_specs=[
            pl.BlockSpec((rows, COL_BLK), lambda c, s: (c, s * N_COL_BLK + BLK_UA)),
            pl.BlockSpec((rows, COL_BLK), lambda c, s: (c, s * N_COL_BLK + BLK_VA)),
            _resident((H_A, CHUNK, CHUNK)),
            _resident((H_A, CHUNK, 1)),
        ],
        out_specs=pl.BlockSpec((rows, D_A), lambda c, s: (c, s)),
        out_shape=jax.ShapeDtypeStruct((t, n * D_A), F32),
        compiler_params=_params(("parallel", "parallel")),
        name="mixer_a_prompt",
    )(proj2, proj2, wm, bm)
    return out.reshape(t * n, D_A)


def _mixer_a_vpu_kernel(u_ref, v_ref, wv_ref, bv_ref, o_ref, *, t):
    for i in range(t):
        mixed = wv_ref[i, 0] * v_ref[0]
        for s in range(1, i + 1):
            mixed = mixed + wv_ref[i, s] * v_ref[s]
        o_ref[i] = u_ref[i] * (mixed + bv_ref[i])


def _mixer_a_sample(proj, t, n, wv, bv):
    proj3 = proj.reshape(t, n, IN_COLS)
    out = pl.pallas_call(
        functools.partial(_mixer_a_vpu_kernel, t=t),
        grid=(1,),
        in_specs=[
            pl.BlockSpec((t, n, COL_BLK), lambda i: (0, 0, BLK_UA)),
            pl.BlockSpec((t, n, COL_BLK), lambda i: (0, 0, BLK_VA)),
            _resident((t, t, 1, D_A)),
            _resident((t, 1, D_A)),
        ],
        out_specs=_resident((t, n, D_A)),
        out_shape=jax.ShapeDtypeStruct((t, n, D_A), F32),
        compiler_params=_params(("arbitrary",)),
        name="mixer_a_sample",
    )(proj3, proj3, wv, bv)
    return out.reshape(t * n, D_A)


def _lru_kernel(xb_ref, gb_ref, conv0_ref, h0_ref, cw_ref, cb_ref, wax_ref, ba_ref, bx_ref, lam_ref,
                yb_ref, convn_ref, hlast_ref, xp_scr, a_scr, d_scr, h_scr, *, tt, n):
    @pl.when(pl.program_id(0) == 0)
    def _():
        xp_scr[0:CONV_W - 1] = conv0_ref[...]
        h_scr[...] = h0_ref[...]

    xp_scr[CONV_W - 1:CONV_W - 1 + tt] = xb_ref[...]
    xc = cb_ref[...] + xp_scr[0:tt] * cw_ref[0]
    for k in range(1, CONV_W):
        xc = xc + xp_scr[k:k + tt] * cw_ref[k]
    tail = xp_scr[tt:tt + CONV_W - 1]
    convn_ref[...] = tail
    xp_scr[0:CONV_W - 1] = tail

    xc2 = xc.reshape(tt * n, D_B)
    pre = _bdot(xc2, wax_ref[...])
    r = jax.nn.sigmoid(pre[:, :D_B] + ba_ref[...])
    i = jax.nn.sigmoid(pre[:, D_B:] + bx_ref[...])
    log_a = -LRU_C * r * jax.nn.softplus(-lam_ref[...])
    a_scr[...] = jnp.exp(log_a).reshape(tt, n, D_B)
    gain = jnp.sqrt(-jnp.tanh(log_a) * (jnp.exp(2.0 * log_a) + 1.0))
    d_scr[...] = (gain * (i * xc2)).reshape(tt, n, D_B)

    def step(s, h):
        h = a_scr[s] * h + d_scr[s]
        d_scr[s] = h
        return h

    h = lax.fori_loop(0, tt, step, h_scr[...], unroll=min(tt, 8))
    h_scr[...] = h
    hlast_ref[...] = h
    yb_ref[...] = d_scr[...] * gb_ref[...]


def _mixer_b(proj, t, n, conv0, h0, cw, cb, wax, ba, bx, lam):
    tt = min(t, 512 // n) if n <= 512 else 1
    proj3 = proj.reshape(t, n, IN_COLS)
    yb, conv_new, h_last = pl.pallas_call(
        functools.partial(_lru_kernel, tt=tt, n=n),
        grid=(t // tt,),
        in_specs=[
            pl.BlockSpec((tt, n, COL_BLK), lambda i: (i, 0, BLK_XB)),
            pl.BlockSpec((tt, n, COL_BLK), lambda i: (i, 0, BLK_GB)),
            _resident((CONV_W - 1, n, D_B)),
            _resident((n, D_B)),
            _resident((CONV_W, 1, D_B)),
            _resident((1, D_B)),
            _resident((D_B, 2 * D_B)),
            _resident((1, D_B)),
            _resident((1, D_B)),
            _resident((1, D_B)),
        ],
        out_specs=[
            pl.BlockSpec((tt, n, D_B), lambda i: (i, 0, 0)),
            _resident((CONV_W - 1, n, D_B)),
            _resident((n, D_B)),
        ],
        out_shape=[
            jax.ShapeDtypeStruct((t, n, D_B), F32),
            jax.ShapeDtypeStruct((CONV_W - 1, n, D_B), F32),
            jax.ShapeDtypeStruct((n, D_B), F32),
        ],
        scratch_shapes=[
            pltpu.VMEM((tt + CONV_W - 1, n, D_B), F32),
            pltpu.VMEM((tt, n, D_B), F32),
            pltpu.VMEM((tt, n, D_B), F32),
            pltpu.VMEM((n, D_B), F32),
        ],
        compiler_params=_params(("arbitrary",)),
        name="mixer_b",
    )(proj3, proj3, conv0, h0, cw, cb, wax, ba, bx, lam)
    return yb.reshape(t * n, D_B), conv_new, h_last


def _s5_disc_kernel(lr_ref, li_ref, ldt_ref, bre_ref, bim_ref, ar_ref, ai_ref, bbr_ref, bbi_ref):
    lr = lr_ref[...]
    li = li_ref[...]
    dt = jnp.exp(ldt_ref[...])
    mag = jnp.exp(lr * dt)
    ar = mag * jnp.cos(li * dt)
    ai = mag * jnp.sin(li * dt)
    ar_ref[...] = ar
    ai_ref[...] = ai
    den = lr * lr + li * li
    qr = ((ar - 1.0) * lr + ai * li) / den
    qi = (ai * lr - (ar - 1.0) * li) / den
    bre = bre_ref[...]
    bim = bim_ref[...]
    bbr_ref[...] = qr * bre - qi * bim
    bbi_ref[...] = qr * bim + qi * bre


def _s5_discretize(lam_re, lam_im, log_dt, b_re, b_im):
    g3 = jax.ShapeDtypeStruct((G_C, 1, P_STATE), F32)
    b3 = jax.ShapeDtypeStruct((G_C, HG_C, P_STATE), F32)
    return pl.pallas_call(
        _s5_disc_kernel,
        out_shape=[g3, g3, b3, b3],
        name="s5_discretize",
    )(lam_re.reshape(G_C, 1, P_STATE), lam_im.reshape(G_C, 1, P_STATE), log_dt.reshape(G_C, 1, 1),
      jnp.swapaxes(b_re, 1, 2), jnp.swapaxes(b_im, 1, 2))


def _s5_kernel(u_ref, s0_ref, ar_ref, ai_ref, bd_ref, cd_ref, d_ref, gw_ref, gb_ref,
               yc_ref, slast_ref, bu_scr, st_scr, *, tt, n):
    @pl.when(pl.program_id(0) == 0)
    def _():
        st_scr[...] = s0_ref[...]

    u2 = u_ref[...].reshape(tt * n, D_C)
    ub = u2.astype(BF16)
    for hf in range(S5_HALVES):
        bu = jnp.dot(ub[:, hf * S5_HALF_IN:(hf + 1) * S5_HALF_IN], bd_ref[hf], preferred_element_type=F32)
        bu_scr[:, :, hf * S5_HALF_ST:(hf + 1) * S5_HALF_ST] = bu.reshape(tt, n, S5_HALF_ST)

    lane_blk = 4 * LANE_TILE
    for hf in range(S5_HALVES):
        for q in range(S5_HALF_RI // lane_blk):
            lr = hf * S5_HALF_ST + q * lane_blk
            li = lr + S5_HALF_RI
            la = hf * S5_HALF_RI + q * lane_blk
            ar = jnp.broadcast_to(ar_ref[:, la:la + lane_blk], (SUBLANE_TILE, lane_blk))
            ai = jnp.broadcast_to(ai_ref[:, la:la + lane_blk], (SUBLANE_TILE, lane_blk))

            def rows_body(rg, carry, lr=lr, li=li, ar=ar, ai=ai):
                rs = pl.ds(pl.multiple_of(rg * SUBLANE_TILE, SUBLANE_TILE), SUBLANE_TILE)

                def step(s, x):
                    xr, xi = x
                    nr = ar * xr - ai * xi + bu_scr[s, rs, lr:lr + lane_blk]
                    ni = ar * xi + ai * xr + bu_scr[s, rs, li:li + lane_blk]
                    bu_scr[s, rs, lr:lr + lane_blk] = nr
                    bu_scr[s, rs, li:li + lane_blk] = ni
                    return nr, ni

                x0 = (st_scr[rs, lr:lr + lane_blk], st_scr[rs, li:li + lane_blk])
                xr, xi = lax.fori_loop(0, tt, step, x0, unroll=min(tt, 8))
                st_scr[rs, lr:lr + lane_blk] = xr
                st_scr[rs, li:li + lane_blk] = xi
                return carry

            lax.fori_loop(0, n // SUBLANE_TILE, rows_body, 0)

    slast_ref[...] = st_scr[...]
    ys = []
    for hf in range(S5_HALVES):
        xs = bu_scr[:, :, hf * S5_HALF_ST:(hf + 1) * S5_HALF_ST].reshape(tt * n, S5_HALF_ST)
        ys.append(_bdot(xs, cd_ref[hf]))
    y = jnp.concatenate(ys, axis=-1) + d_ref[...] * u2
    z = jax.nn.gelu(y)
    yc = z * jax.nn.sigmoid(_bdot(z, gw_ref[...]) + gb_ref[...])
    yc_ref[...] = yc.reshape(tt, n, D_C)


def _mixer_c(proj, t, n, s0, ar, ai, bd, cd, d, glu_w, glu_b):
    tt = min(t, 512 // n) if n <= 512 else 1
    proj3 = proj.reshape(t, n, IN_COLS)
    yc, s_last = pl.pallas_call(
        functools.partial(_s5_kernel, tt=tt, n=n),
        grid=(t // tt,),
        in_specs=[
            pl.BlockSpec((tt, n, COL_BLK), lambda i: (i, 0, BLK_UC)),
            _resident((n, S5_LANES)),
            _resident((1, S5_LANES // 2)),
            _resident((1, S5_LANES // 2)),
            _resident((S5_HALVES, S5_HALF_IN, S5_HALF_ST)),
            _resident((S5_HALVES, S5_HALF_ST, S5_HALF_IN)),
            _resident((1, D_C)),
            _resident((D_C, D_C)),
            _resident((1, D_C)),
        ],
        out_specs=[
            pl.BlockSpec((tt, n, D_C), lambda i: (i, 0, 0)),
            _resident((n, S5_LANES)),
        ],
        out_shape=[
            jax.ShapeDtypeStruct((t, n, D_C), F32),
            jax.ShapeDtypeStruct((n, S5_LANES), F32),
        ],
        scratch_shapes=[
            pltpu.VMEM((tt, n, S5_LANES), F32),
            pltpu.VMEM((n, S5_LANES), F32),
        ],
        compiler_params=_params(("arbitrary",)),
        name="mixer_c",
    )(proj3, s0, ar, ai, bd, cd, d, glu_w, glu_b)
    return yc.reshape(t * n, D_C), s_last


def _single(shape):
    zeros = (0,) * len(shape)
    return pl.BlockSpec(shape, lambda *_: zeros, pipeline_mode=pl.Buffered(1))


def _merge_kernel(ya_ref, yb_ref, yc_ref, g0, g1, g2, g3, g4, g5, x_ref, wbr_ref, wout_ref, o_ref):
    g_refs = (g0, g1, g2, g3, g4, g5)
    m = None
    for k, y_ref in enumerate((ya_ref, yb_ref, yc_ref)):
        p = _bdot(y_ref[...], wbr_ref[k])
        g = jnp.concatenate([g_refs[2 * k][...], g_refs[2 * k + 1][...]], axis=-1)
        m = g * p if m is None else m + g * p
    o_ref[...] = x_ref[...] + _bdot(m, wout_ref[...])


def _merge(ya, yb, yc, proj, x, wbr, wout):
    r = x.shape[0]
    tm = min(r, 512)
    row = pl.BlockSpec((tm, D_MODEL), lambda i: (i, 0))
    gate_specs = [pl.BlockSpec((tm, COL_BLK), functools.partial(lambda c, i: (i, c), BLK_GATE + k))
                  for k in range(2 * N_BRANCH)]
    return pl.pallas_call(
        _merge_kernel,
        grid=(r // tm,),
        in_specs=[pl.BlockSpec((tm, D_BR), lambda i: (i, 0))] * N_BRANCH + gate_specs + [
            row,
            _single((N_BRANCH, D_BR, D_MODEL)),
            _single((D_MODEL, D_MODEL)),
        ],
        out_specs=row,
        out_shape=jax.ShapeDtypeStruct((r, D_MODEL), F32),
        compiler_params=_params(("parallel",)),
        name="merge",
    )(ya, yb, yc, *([proj] * (2 * N_BRANCH)), x, wbr, wout)


def _dense_ffn_kernel(x_ref, nf_ref, wg_ref, wu_ref, wd_ref, fin_ref, o_ref, *, final):
    x1 = x_ref[...]
    hb = _rms(x1, nf_ref[...]).astype(BF16)
    ffn = None
    for c in range(D_FF // D_EXP):
        cols = slice(c * D_EXP, (c + 1) * D_EXP)
        gate = jnp.dot(hb, wg_ref[:, cols], preferred_element_type=F32)
        up = jnp.dot(hb, wu_ref[:, cols], preferred_element_type=F32)
        part = _bdot(jax.nn.silu(gate) * up, wd_ref[cols, :])
        ffn = part if ffn is None else ffn + part
    x2 = x1 + ffn
    o_ref[...] = _rms(x2, fin_ref[...]) if final else x2


def _dense_ffn(x1, nf, wg, wu, wd, fin, final):
    r = x1.shape[0]
    tm = min(r, 512)
    row = pl.BlockSpec((tm, D_MODEL), lambda i: (i, 0))
    return pl.pallas_call(
        functools.partial(_dense_ffn_kernel, final=final),
        grid=(r // tm,),
        in_specs=[
            row,
            _resident((1, D_MODEL)),
            _single((D_MODEL, D_FF)),
            _single((D_MODEL, D_FF)),
            _single((D_FF, D_MODEL)),
            _resident((1, D_MODEL)),
        ],
        out_specs=row,
        out_shape=jax.ShapeDtypeStruct((r, D_MODEL), F32),
        compiler_params=_params(("parallel",)),
        name="dense_ffn",
    )(x1, nf, wg, wu, wd, fin)


def _moe_kernel(x_ref, nf_ref, rw_ref, rb_ref, wg_ref, wu_ref, wd_ref, fin_ref, o_ref,
                hb_scr, comb_scr, acc_scr, *, final):
    e = pl.program_id(1)
    lane = lax.broadcasted_iota(jnp.int32, comb_scr.shape, 1)

    @pl.when(e == 0)
    def _():
        hn = _rms(x_ref[...], nf_ref[...])
        hb_scr[...] = hn.astype(BF16)
        logits = jnp.dot(hn, rw_ref[...], preferred_element_type=F32,
                         precision=lax.Precision.HIGHEST) + rb_ref[...]
        neg = jnp.float32(-jnp.inf)
        lg = jnp.where(lane < N_EXPERTS, logits, neg)
        m1 = jnp.max(lg, axis=-1, keepdims=True)
        i1 = jnp.min(jnp.where(lg == m1, lane, LANE_TILE), axis=-1, keepdims=True)
        lg2 = jnp.where(lane == i1, neg, lg)
        m2 = jnp.max(lg2, axis=-1, keepdims=True)
        i2 = jnp.min(jnp.where(lg2 == m2, lane, LANE_TILE), axis=-1, keepdims=True)
        e2 = jnp.exp(m2 - m1)
        den = 1.0 + e2
        comb_scr[...] = jnp.where(lane == i1, 1.0 / den, 0.0) + jnp.where(lane == i2, e2 / den, 0.0)
        acc_scr[...] = jnp.zeros_like(acc_scr)

    hb = hb_scr[...]
    gate = jnp.dot(hb, wg_ref[...], preferred_element_type=F32)
    up = jnp.dot(hb, wu_ref[...], preferred_element_type=F32)
    out_e = _bdot(jax.nn.silu(gate) * up, wd_ref[...])
    w_e = jnp.sum(jnp.where(lane == e, comb_scr[...], 0.0), axis=-1, keepdims=True)
    acc_scr[...] += w_e * out_e

    @pl.when(e == N_EXPERTS - 1)
    def _():
        x2 = x_ref[...] + acc_scr[...]
        o_ref[...] = _rms(x2, fin_ref[...]) if final else x2


def _moe_ffn(x1, nf, rw, rb, wg, wu, wd, fin, final):
    r = x1.shape[0]
    tm = min(r, 512)
    row = pl.BlockSpec((tm, D_MODEL), lambda i, e: (i, 0))
    return pl.pallas_call(
        functools.partial(_moe_kernel, final=final),
        grid=(r // tm, N_EXPERTS),
        in_specs=[
            row,
            _resident((1, D_MODEL)),
            _single((D_MODEL, LANE_TILE)),
            _resident((1, LANE_TILE)),
            pl.BlockSpec((None, D_MODEL, D_EXP), lambda i, e: (e, 0, 0)),
            pl.BlockSpec((None, D_MODEL, D_EXP), lambda i, e: (e, 0, 0)),
            pl.BlockSpec((None, D_EXP, D_MODEL), lambda i, e: (e, 0, 0)),
            _resident((1, D_MODEL)),
        ],
        out_specs=row,
        out_shape=jax.ShapeDtypeStruct((r, D_MODEL), F32),
        scratch_shapes=[
            pltpu.VMEM((tm, D_MODEL), BF16),
            pltpu.VMEM((tm, LANE_TILE), F32),
            pltpu.VMEM((tm, D_MODEL), F32),
        ],
        compiler_params=_params(("parallel", "arbitrary")),
        name="moe_ffn",
    )(x1, nf, rw, rb, wg, wu, wd, fin)


def _block_diag(w):
    h, i, j = w.shape
    eye = jnp.eye(h, dtype=w.dtype)
    return jnp.einsum("hij,hk->hikj", w, eye).reshape(h * i, h * j)


def _s5_in_matrix(bbr, bbi):
    gh = G_C // S5_HALVES
    out = []
    for part in (bbr, bbi):
        p4 = part.reshape(S5_HALVES, gh, HG_C, P_STATE)
        eye = jnp.eye(gh, dtype=part.dtype)
        out.append(jnp.einsum("aghp,gk->aghkp", p4, eye).reshape(S5_HALVES, gh * HG_C, gh * P_STATE))
    return jnp.concatenate(out, axis=-1).astype(BF16)


def _s5_out_matrix(c_re, c_im):
    gh = G_C // S5_HALVES
    out = []
    for part in (c_re, -c_im):
        p4 = part.reshape(S5_HALVES, gh, HG_C, P_STATE)
        eye = jnp.eye(gh, dtype=part.dtype)
        out.append(jnp.einsum("agop,gk->agpko", p4, eye).reshape(S5_HALVES, gh * P_STATE, gh * HG_C))
    return jnp.concatenate(out, axis=1).astype(BF16)


def _s5_state_in(re, im):
    n = re.shape[0]
    parts = [a.reshape(n, S5_HALVES, 1, S5_HALF_RI) for a in (re, im)]
    return jnp.concatenate(parts, axis=2).reshape(n, S5_LANES)


def _s5_state_out(s):
    n = s.shape[0]
    s4 = s.reshape(n, S5_HALVES, 2, S5_HALF_RI)
    return s4[:, :, 0].reshape(n, G_C, P_STATE), s4[:, :, 1].reshape(n, G_C, P_STATE)


def _lane_vec(v):
    return v.reshape(1, -1)


def _layer_weights(l, w):
    ar, ai, bbr, bbi = _s5_discretize(w["c_lam_re"][l], w["c_lam_im"][l], w["c_log_dt"][l],
                                      w["c_b_re"][l], w["c_b_im"][l])
    causal = jnp.tril(jnp.ones((CHUNK, CHUNK), dtype=bool))
    ws = w["a_ws"][l]
    lw = dict(
        norm_mix_g=_lane_vec(w["norm_mix_g"][l]),
        w_in=w["w_in"][l].astype(BF16),
        b_gate=_lane_vec(w["b_gate"][l]),
        a_ln_g=_lane_vec(w["a_ln_g"][l]),
        a_ln_b=_lane_vec(w["a_ln_b"][l]),
        a_wm=jnp.where(causal[None], ws, jnp.zeros_like(ws)).astype(BF16),
        a_bm=w["a_bs"][l][:, :, None],
        a_ws=ws,
        a_bs=w["a_bs"][l],
        b_cw=w["b_conv_w"][l][:, None, :],
        b_cb=_lane_vec(w["b_conv_b"][l]),
        b_wax=jnp.concatenate([_block_diag(w["b_wa"][l]), _block_diag(w["b_wx"][l])], axis=1).astype(BF16),
        b_ba=_lane_vec(w["b_ba"][l]),
        b_bx=_lane_vec(w["b_bx"][l]),
        b_lam=_lane_vec(w["b_lam"][l]),
        c_ar=ar.reshape(S5_HALVES, S5_HALF_RI).reshape(1, S5_LANES // 2),
        c_ai=ai.reshape(S5_HALVES, S5_HALF_RI).reshape(1, S5_LANES // 2),
        c_bd=_s5_in_matrix(bbr, bbi),
        c_cd=_s5_out_matrix(w["c_c_re"][l], w["c_c_im"][l]),
        c_d=_lane_vec(w["c_d"][l]),
        c_glu_w=w["c_glu_w"][l].astype(BF16),
        c_glu_b=_lane_vec(w["c_glu_b"][l]),
        w_branch=w["w_branch"][l].astype(BF16),
        w_out=w["w_out"][l].astype(BF16),
        norm_ffn_g=_lane_vec(w["norm_ffn_g"][l]),
        final_norm_g=_lane_vec(w["final_norm_g"]),
    )
    j = l // 2
    if l % 2 == 0:
        lw.update(ffn_wg=w["ffn_w_gate"][j].astype(BF16), ffn_wu=w["ffn_w_up"][j].astype(BF16),
                  ffn_wd=w["ffn_w_down"][j].astype(BF16))
    else:
        pad = LANE_TILE - N_EXPERTS
        lw.update(moe_rw=jnp.pad(w["moe_router_w"][j], ((0, 0), (0, pad))),
                  moe_rb=jnp.pad(w["moe_router_b"][j], (0, pad)).reshape(1, LANE_TILE),
                  moe_wg=w["moe_w_gate"][j].astype(BF16), moe_wu=w["moe_w_up"][j].astype(BF16),
                  moe_wd=w["moe_w_down"][j].astype(BF16))
    return lw


def _trunk(x_tm, t, n, conv_s, lru_s, s5re_s, s5im_s, layers):
    x = x_tm
    vas, convs, lrus, sres, sims = [], [], [], [], []
    for l, lw in enumerate(layers):
        proj = _in_proj(x, lw["norm_mix_g"], lw["w_in"], lw["b_gate"], lw["a_ln_g"], lw["a_ln_b"])
        if t % (4 * CHUNK) == 0:
            ya = _mixer_a_prompt(proj, t, n, lw["a_wm"], lw["a_bm"])
        else:
            assert t < CHUNK
            wv = jnp.repeat(jnp.transpose(lw["a_ws"][:, :t, :t], (1, 2, 0)), HD_A, axis=-1)[:, :, None, :]
            bv = jnp.repeat(jnp.transpose(lw["a_bs"][:, :t], (1, 0)), HD_A, axis=-1)[:, None, :]
            ya = _mixer_a_sample(proj, t, n, wv, bv)
        yb, conv_new, h_last = _mixer_b(proj, t, n, jnp.swapaxes(conv_s[l], 0, 1), lru_s[l], lw["b_cw"],
                                        lw["b_cb"], lw["b_wax"], lw["b_ba"], lw["b_bx"], lw["b_lam"])
        yc, s_last = _mixer_c(proj, t, n, _s5_state_in(s5re_s[l], s5im_s[l]), lw["c_ar"], lw["c_ai"],
                              lw["c_bd"], lw["c_cd"], lw["c_d"], lw["c_glu_w"], lw["c_glu_b"])
        final = l == len(layers) - 1
        x1 = _merge(ya, yb, yc, proj, x, lw["w_branch"], lw["w_out"])
        if l % 2 == 0:
            x = _dense_ffn(x1, lw["norm_ffn_g"], lw["ffn_wg"], lw["ffn_wu"], lw["ffn_wd"],
                           lw["final_norm_g"], final)
        else:
            x = _moe_ffn(x1, lw["norm_ffn_g"], lw["moe_rw"], lw["moe_rb"], lw["moe_wg"], lw["moe_wu"],
                         lw["moe_wd"], lw["final_norm_g"], final)
        va = proj.reshape(t, n, IN_COLS)[:, :, BLK_VA * COL_BLK:(BLK_VA + 1) * COL_BLK]
        vas.append(jnp.swapaxes(va, 0, 1))
        convs.append(jnp.swapaxes(conv_new, 0, 1))
        lrus.append(h_last)
        s_re, s_im = _s5_state_out(s_last)
        sres.append(s_re)
        sims.append(s_im)
    return x, jnp.stack(vas), jnp.stack(convs), jnp.stack(lrus), jnp.stack(sres), jnp.stack(sims)


def kernel(x_prompt, x_sample, state_conv_b, state_lru_b, state_s5_re, state_s5_im, norm_mix_g, w_in, b_gate, a_ln_g, a_ln_b, a_ws, a_bs, b_conv_w, b_conv_b, b_wa, b_ba, b_wx, b_bx, b_lam, c_lam_re, c_lam_im, c_log_dt, c_b_re, c_b_im, c_c_re, c_c_im, c_d, c_glu_w, c_glu_b, w_branch, w_out, norm_ffn_g, ffn_w_gate, ffn_w_up, ffn_w_down, moe_router_w, moe_router_b, moe_w_gate, moe_w_up, moe_w_down, final_norm_g):
    w = dict(norm_mix_g=norm_mix_g, w_in=w_in, b_gate=b_gate, a_ln_g=a_ln_g, a_ln_b=a_ln_b, a_ws=a_ws,
             a_bs=a_bs, b_conv_w=b_conv_w, b_conv_b=b_conv_b, b_wa=b_wa, b_ba=b_ba, b_wx=b_wx, b_bx=b_bx,
             b_lam=b_lam, c_lam_re=c_lam_re, c_lam_im=c_lam_im, c_log_dt=c_log_dt, c_b_re=c_b_re,
             c_b_im=c_b_im, c_c_re=c_c_re, c_c_im=c_c_im, c_d=c_d, c_glu_w=c_glu_w, c_glu_b=c_glu_b,
             w_branch=w_branch, w_out=w_out, norm_ffn_g=norm_ffn_g, ffn_w_gate=ffn_w_gate,
             ffn_w_up=ffn_w_up, ffn_w_down=ffn_w_down, moe_router_w=moe_router_w,
             moe_router_b=moe_router_b, moe_w_gate=moe_w_gate, moe_w_up=moe_w_up, moe_w_down=moe_w_down,
             final_norm_g=final_norm_g)
    layers = [_layer_weights(l, w) for l in range(DEPTH)]

    nb, tp, _ = x_prompt.shape
    ns, ts, _ = x_sample.shape
    conv0 = jnp.zeros((DEPTH, nb, CONV_W - 1, D_B), F32)
    lru0 = jnp.zeros((DEPTH, nb, D_B), F32)
    s50 = jnp.zeros((DEPTH, nb, G_C, P_STATE), F32)
    xp = jnp.swapaxes(x_prompt, 0, 1).reshape(tp * nb, D_MODEL)
    yp, _, conv_p, lru_p, sre_p, sim_p = _trunk(xp, tp, nb, conv0, lru0, s50, s50, layers)
    xs = jnp.swapaxes(x_sample, 0, 1).reshape(ts * ns, D_MODEL)
    ys, va_s, conv_s, lru_s, sre_s, sim_s = _trunk(xs, ts, ns, state_conv_b, state_lru_b, state_s5_re,
                                                   state_s5_im, layers)
    y_prompt = jnp.swapaxes(yp.reshape(tp, nb, D_MODEL), 0, 1)
    y_sample = jnp.swapaxes(ys.reshape(ts, ns, D_MODEL), 0, 1)
    return (y_prompt, y_sample, conv_p, lru_p, sre_p, sim_p, va_s, conv_s, lru_s, sre_s, sim_s)
```

```python
import functools
import math

import jax
import jax.numpy as jnp
from jax import lax
from jax.experimental import pallas as pl
from jax.experimental.pallas import tpu as pltpu

D_MODEL = 1024
DEPTH = 2
CHUNK = 128
D_A = D_MODEL // 2
H_A = 4
HD_A = D_A // H_A
D_B = D_MODEL // 2
H_B = 8
BH_B = D_B // H_B
CONV_W = 4
LRU_C = 8.0
D_C = D_MODEL // 2
HG_C = 16
G_C = D_C // HG_C
P_STATE = 64
N_BRANCH = 3
D_BR = D_MODEL // 2
IN_COLS = 2 * D_A + 2 * D_B + D_C + N_BRANCH * D_MODEL
D_FF = 11 * D_MODEL // 4
N_EXPERTS = 8
TOP_K = 2
D_EXP = D_FF // 2
EPS = 1e-6

F32 = jnp.float32
BF16 = jnp.bfloat16

COL_BLK = 512
N_COL_BLK = IN_COLS // COL_BLK
BLK_UA, BLK_VA, BLK_XB, BLK_GB, BLK_UC, BLK_GATE = 0, 1, 2, 3, 4, 5
PA_COLS = BLK_GATE * COL_BLK
S5_HALVES = 2
S5_HALF_IN = D_C // S5_HALVES
S5_HALF_RI = (G_C // S5_HALVES) * P_STATE
S5_HALF_ST = 2 * S5_HALF_RI
S5_LANES = S5_HALVES * S5_HALF_ST
LANE_TILE = 128
SUBLANE_TILE = 8
MXU_DIM = 256
VMEM_LIMIT = 56 * 1024 * 1024


def _resident(shape):
    zeros = (0,) * len(shape)
    return pl.BlockSpec(shape, lambda *_: zeros)


def _params(sem):
    return pltpu.CompilerParams(dimension_semantics=sem, vmem_limit_bytes=VMEM_LIMIT)


def _rms(x, g):
    return x * lax.rsqrt(jnp.mean(x * x, axis=-1, keepdims=True) + EPS) * g


def _bdot(a, b):
    return jnp.dot(a.astype(BF16), b, preferred_element_type=F32)


def _in_proj_kernel(x_ref, g_ref, w_ref, bg_ref, lng_ref, lnb_ref, pa_ref, gate_ref):
    hb = _rms(x_ref[...], g_ref[...]).astype(BF16)
    for j in range(N_COL_BLK):
        cols = slice(j * COL_BLK, (j + 1) * COL_BLK)
        acc = jnp.dot(hb, w_ref[:, cols], preferred_element_type=F32)
        if j in (BLK_UA, BLK_GB):
            pa_ref[:, cols] = jax.nn.gelu(acc)
        elif j == BLK_VA:
            v = jax.nn.gelu(acc)
            vc = v - jnp.mean(v, axis=-1, keepdims=True)
            var = jnp.mean(vc * vc, axis=-1, keepdims=True)
            pa_ref[:, cols] = vc * lax.rsqrt(var + EPS) * lng_ref[...] + lnb_ref[...]
        elif j in (BLK_XB, BLK_UC):
            pa_ref[:, cols] = acc
        else:
            gcols = slice(cols.start - PA_COLS, cols.stop - PA_COLS)
            gate_ref[:, gcols] = jax.nn.sigmoid(acc + bg_ref[:, gcols]).astype(BF16)


def _in_proj(x, norm_g, w_in, b_gate, ln_g, ln_b):
    r = x.shape[0]
    tm = min(r, 512)
    return pl.pallas_call(
        _in_proj_kernel,
        grid=(r // tm,),
        in_specs=[
            pl.BlockSpec((tm, D_MODEL), lambda i: (i, 0)),
            _resident((1, D_MODEL)),
            _single((D_MODEL, IN_COLS)),
            _resident((1, N_BRANCH * D_MODEL)),
            _resident((1, D_A)),
            _resident((1, D_A)),
        ],
        out_specs=[
            pl.BlockSpec((tm, PA_COLS), lambda i: (i, 0)),
            pl.BlockSpec((tm, N_BRANCH * D_MODEL), lambda i: (i, 0)),
        ],
        out_shape=[
            jax.ShapeDtypeStruct((r, PA_COLS), F32),
            jax.ShapeDtypeStruct((r, N_BRANCH * D_MODEL), BF16),
        ],
        compiler_params=_params(("parallel",)),
        name="in_proj",
    )(x, norm_g, w_in, b_gate, ln_g, ln_b)


def _mixer_a_mxu_kernel(u_ref, v_ref, wk_ref, bk_ref, o_ref, *, rows):
    vb = v_ref[...].astype(BF16)
    for h in range(H_A):
        cols = slice(h * HD_A, (h + 1) * HD_A)
        for r0 in range(0, rows, MXU_DIM):
            k = r0 + MXU_DIM
            mixed = jnp.dot(wk_ref[h, r0:k, 0:k], vb[0:k, cols], preferred_element_type=F32)
            o_ref[r0:k, cols] = u_ref[r0:k, cols] * (mixed + bk_ref[r0:k, cols])


def _mixer_a_prompt(pa, t, n, wk, bk):
    rows = CHUNK * n
    assert rows % MXU_DIM == 0 and MXU_DIM % n == 0
    return pl.pallas_call(
        functools.partial(_mixer_a_mxu_kernel, rows=rows),
        grid=(t // CHUNK,),
        in_specs=[
            pl.BlockSpec((rows, COL_BLK), lambda c: (c, BLK_UA)),
            pl.BlockSpec((rows, COL_BLK), lambda c: (c, BLK_VA)),
            _single((H_A, rows, rows)),
            _single((rows, D_A)),
        ],
        out_specs=pl.BlockSpec((rows, D_A), lambda c: (c, 0)),
        out_shape=jax.ShapeDtypeStruct((t * n, D_A), F32),
        compiler_params=_params(("parallel",)),
        name="mixer_a_prompt",
    )(pa, pa, wk, bk)


def _mixer_a_vpu_kernel(u_ref, v_ref, wv_ref, bv_ref, o_ref, *, t):
    for i in range(t):
        mixed = wv_ref[i, 0] * v_ref[0]
        for s in range(1, i + 1):
            mixed = mixed + wv_ref[i, s] * v_ref[s]
        o_ref[i] = u_ref[i] * (mixed + bv_ref[i])


def _mixer_a_sample(proj, t, n, wv, bv):
    proj3 = proj.reshape(t, n, PA_COLS)
    out = pl.pallas_call(
        functools.partial(_mixer_a_vpu_kernel, t=t),
        grid=(1,),
        in_specs=[
            pl.BlockSpec((t, n, COL_BLK), lambda i: (0, 0, BLK_UA)),
            pl.BlockSpec((t, n, COL_BLK), lambda i: (0, 0, BLK_VA)),
            _resident((t, t, 1, D_A)),
            _resident((t, 1, D_A)),
        ],
        out_specs=_resident((t, n, D_A)),
        out_shape=jax.ShapeDtypeStruct((t, n, D_A), F32),
        compiler_params=_params(("arbitrary",)),
        name="mixer_a_sample",
    )(proj3, proj3, wv, bv)
    return out.reshape(t * n, D_A)


def _lru_kernel(xb_ref, gb_ref, conv0_ref, h0_ref, cw_ref, cb_ref, wax_ref, ba_ref, bx_ref, lam_ref,
                yb_ref, convn_ref, hlast_ref, xp_scr, a_scr, d_scr, h_scr, *, tt, n):
    @pl.when(pl.program_id(0) == 0)
    def _():
        xp_scr[0:CONV_W - 1] = conv0_ref[...]
        h_scr[...] = h0_ref[...]

    xp_scr[CONV_W - 1:CONV_W - 1 + tt] = xb_ref[...]
    xc = cb_ref[...] + xp_scr[0:tt] * cw_ref[0]
    for k in range(1, CONV_W):
        xc = xc + xp_scr[k:k + tt] * cw_ref[k]
    tail = xp_scr[tt:tt + CONV_W - 1]
    convn_ref[...] = tail
    xp_scr[0:CONV_W - 1] = tail

    xc2 = xc.reshape(tt * n, D_B)
    pre = _bdot(xc2, wax_ref[...])
    r = jax.nn.sigmoid(pre[:, :D_B] + ba_ref[...])
    i = jax.nn.sigmoid(pre[:, D_B:] + bx_ref[...])
    log_a = -LRU_C * r * jax.nn.softplus(-lam_ref[...])
    a_scr[...] = jnp.exp(log_a).reshape(tt, n, D_B)
    gain = jnp.sqrt(-jnp.tanh(log_a) * (jnp.exp(2.0 * log_a) + 1.0))
    d_scr[...] = (gain * (i * xc2)).reshape(tt, n, D_B)

    def step(s, h):
        h = a_scr[s] * h + d_scr[s]
        d_scr[s] = h
        return h

    h = lax.fori_loop(0, tt, step, h_scr[...], unroll=min(tt, 8))
    h_scr[...] = h
    hlast_ref[...] = h
    yb_ref[...] = d_scr[...] * gb_ref[...]


def _mixer_b(proj, t, n, conv0, h0, cw, cb, wax, ba, bx, lam):
    tt = min(t, 512 // n) if n <= 512 else 1
    proj3 = proj.reshape(t, n, PA_COLS)
    yb, conv_new, h_last = pl.pallas_call(
        functools.partial(_lru_kernel, tt=tt, n=n),
        grid=(t // tt,),
        in_specs=[
            pl.BlockSpec((tt, n, COL_BLK), lambda i: (i, 0, BLK_XB)),
            pl.BlockSpec((tt, n, COL_BLK), lambda i: (i, 0, BLK_GB)),
            _resident((CONV_W - 1, n, D_B)),
            _resident((n, D_B)),
            _resident((CONV_W, 1, D_B)),
            _resident((1, D_B)),
            _resident((D_B, 2 * D_B)),
            _resident((1, D_B)),
            _resident((1, D_B)),
            _resident((1, D_B)),
        ],
        out_specs=[
            pl.BlockSpec((tt, n, D_B), lambda i: (i, 0, 0)),
            _resident((CONV_W - 1, n, D_B)),
            _resident((n, D_B)),
        ],
        out_shape=[
            jax.ShapeDtypeStruct((t, n, D_B), F32),
            jax.ShapeDtypeStruct((CONV_W - 1, n, D_B), F32),
            jax.ShapeDtypeStruct((n, D_B), F32),
        ],
        scratch_shapes=[
            pltpu.VMEM((tt + CONV_W - 1, n, D_B), F32),
            pltpu.VMEM((tt, n, D_B), F32),
            pltpu.VMEM((tt, n, D_B), F32),
            pltpu.VMEM((n, D_B), F32),
        ],
        compiler_params=_params(("arbitrary",)),
        name="mixer_b",
    )(proj3, proj3, conv0, h0, cw, cb, wax, ba, bx, lam)
    return yb.reshape(t * n, D_B), conv_new, h_last


def _s5_disc_kernel(lr_ref, li_ref, ldt_ref, bre_ref, bim_ref, ar_ref, ai_ref, bbr_ref, bbi_ref):
    lr = lr_ref[...]
    li = li_ref[...]
    dt = jnp.exp(ldt_ref[...])
    mag = jnp.exp(lr * dt)
    ar = mag * jnp.cos(li * dt)
    ai = mag * jnp.sin(li * dt)
    ar_ref[...] = ar
    ai_ref[...] = ai
    den = lr * lr + li * li
    qr = ((ar - 1.0) * lr + ai * li) / den
    qi = (ai * lr - (ar - 1.0) * li) / den
    bre = bre_ref[...]
    bim = bim_ref[...]
    bbr_ref[...] = qr * bre - qi * bim
    bbi_ref[...] = qr * bim + qi * bre


def _s5_discretize(lam_re, lam_im, log_dt, b_re, b_im):
    g3 = jax.ShapeDtypeStruct((G_C, 1, P_STATE), F32)
    b3 = jax.ShapeDtypeStruct((G_C, HG_C, P_STATE), F32)
    return pl.pallas_call(
        _s5_disc_kernel,
        out_shape=[g3, g3, b3, b3],
        name="s5_discretize",
    )(lam_re.reshape(G_C, 1, P_STATE), lam_im.reshape(G_C, 1, P_STATE), log_dt.reshape(G_C, 1, 1),
      jnp.swapaxes(b_re, 1, 2), jnp.swapaxes(b_im, 1, 2))


def _s5_kernel(u_ref, s0_ref, ar_ref, ai_ref, bd_ref, cd_ref, d_ref, gw_ref, gb_ref,
               yc_ref, slast_ref, bu_scr, st_scr, *, tt, n):
    @pl.when(pl.program_id(0) == 0)
    def _():
        st_scr[...] = s0_ref[...]

    u2 = u_ref[...].reshape(tt * n, D_C)
    ub = u2.astype(BF16)
    for hf in range(S5_HALVES):
        bu = jnp.dot(ub[:, hf * S5_HALF_IN:(hf + 1) * S5_HALF_IN], bd_ref[hf], preferred_element_type=F32)
        bu_scr[:, :, hf * S5_HALF_ST:(hf + 1) * S5_HALF_ST] = bu.reshape(tt, n, S5_HALF_ST)

    lane_blk = 4 * LANE_TILE
    for hf in range(S5_HALVES):
        for q in range(S5_HALF_RI // lane_blk):
            lr = hf * S5_HALF_ST + q * lane_blk
            li = lr + S5_HALF_RI
            la = hf * S5_HALF_RI + q * lane_blk
            ar = jnp.broadcast_to(ar_ref[:, la:la + lane_blk], (SUBLANE_TILE, lane_blk))
            ai = jnp.broadcast_to(ai_ref[:, la:la + lane_blk], (SUBLANE_TILE, lane_blk))

            def rows_body(rg, carry, lr=lr, li=li, ar=ar, ai=ai):
                rs = pl.ds(pl.multiple_of(rg * SUBLANE_TILE, SUBLANE_TILE), SUBLANE_TILE)

                def step(s, x):
                    xr, xi = x
                    nr = ar * xr - ai * xi + bu_scr[s, rs, lr:lr + lane_blk]
                    ni = ar * xi + ai * xr + bu_scr[s, rs, li:li + lane_blk]
                    bu_scr[s, rs, lr:lr + lane_blk] = nr
                    bu_scr[s, rs, li:li + lane_blk] = ni
                    return nr, ni

                x0 = (st_scr[rs, lr:lr + lane_blk], st_scr[rs, li:li + lane_blk])
                xr, xi = lax.fori_loop(0, tt, step, x0, unroll=min(tt, 8))
                st_scr[rs, lr:lr + lane_blk] = xr
                st_scr[rs, li:li + lane_blk] = xi
                return carry

            lax.fori_loop(0, n // SUBLANE_TILE, rows_body, 0)

    slast_ref[...] = st_scr[...]
    ys = []
    for hf in range(S5_HALVES):
        xs = bu_scr[:, :, hf * S5_HALF_ST:(hf + 1) * S5_HALF_ST].reshape(tt * n, S5_HALF_ST)
        ys.append(_bdot(xs, cd_ref[hf]))
    y = jnp.concatenate(ys, axis=-1) + d_ref[...] * u2
    z = jax.nn.gelu(y)
    yc = z * jax.nn.sigmoid(_bdot(z, gw_ref[...]) + gb_ref[...])
    yc_ref[...] = yc.reshape(tt, n, D_C)


def _mixer_c(proj, t, n, s0, ar, ai, bd, cd, d, glu_w, glu_b):
    tt = min(t, 512 // n) if n <= 512 else 1
    proj3 = proj.reshape(t, n, PA_COLS)
    yc, s_last = pl.pallas_call(
        functools.partial(_s5_kernel, tt=tt, n=n),
        grid=(t // tt,),
        in_specs=[
            pl.BlockSpec((tt, n, COL_BLK), lambda i: (i, 0, BLK_UC)),
            _resident((n, S5_LANES)),
            _resident((1, S5_LANES // 2)),
            _resident((1, S5_LANES // 2)),
            _resident((S5_HALVES, S5_HALF_IN, S5_HALF_ST)),
            _resident((S5_HALVES, S5_HALF_ST, S5_HALF_IN)),
            _resident((1, D_C)),
            _resident((D_C, D_C)),
            _resident((1, D_C)),
        ],
        out_specs=[
            pl.BlockSpec((tt, n, D_C), lambda i: (i, 0, 0)),
            _resident((n, S5_LANES)),
        ],
        out_shape=[
            jax.ShapeDtypeStruct((t, n, D_C), F32),
            jax.ShapeDtypeStruct((n, S5_LANES), F32),
        ],
        scratch_shapes=[
            pltpu.VMEM((tt, n, S5_LANES), F32),
            pltpu.VMEM((n, S5_LANES), F32),
        ],
        compiler_params=_params(("arbitrary",)),
        name="mixer_c",
    )(proj3, s0, ar, ai, bd, cd, d, glu_w, glu_b)
    return yc.reshape(t * n, D_C), s_last


def _single(shape):
    zeros = (0,) * len(shape)
    return pl.BlockSpec(shape, lambda *_: zeros, pipeline_mode=pl.Buffered(1))


def _merge_kernel(ya_ref, yb_ref, yc_ref, g_ref, x_ref, wbr_ref, wout_ref, o_ref):
    m = None
    for k, y_ref in enumerate((ya_ref, yb_ref, yc_ref)):
        p = _bdot(y_ref[...], wbr_ref[k])
        g = g_ref[:, k * D_MODEL:(k + 1) * D_MODEL].astype(F32)
        m = g * p if m is None else m + g * p
    o_ref[...] = x_ref[...] + _bdot(m, wout_ref[...])


def _merge(ya, yb, yc, gates, x, wbr, wout):
    r = x.shape[0]
    tm = min(r, 512)
    row = pl.BlockSpec((tm, D_MODEL), lambda i: (i, 0))
    return pl.pallas_call(
        _merge_kernel,
        grid=(r // tm,),
        in_specs=[pl.BlockSpec((tm, D_BR), lambda i: (i, 0))] * N_BRANCH + [
            pl.BlockSpec((tm, N_BRANCH * D_MODEL), lambda i: (i, 0)),
            row,
            _single((N_BRANCH, D_BR, D_MODEL)),
            _single((D_MODEL, D_MODEL)),
        ],
        out_specs=row,
        out_shape=jax.ShapeDtypeStruct((r, D_MODEL), F32),
        compiler_params=_params(("parallel",)),
        name="merge",
    )(ya, yb, yc, gates, x, wbr, wout)


def _dense_ffn_kernel(x_ref, nf_ref, wg_ref, wu_ref, wd_ref, fin_ref, o_ref, *, final):
    x1 = x_ref[...]
    hb = _rms(x1, nf_ref[...]).astype(BF16)
    ffn = None
    for c in range(D_FF // D_EXP):
        cols = slice(c * D_EXP, (c + 1) * D_EXP)
        gate = jnp.dot(hb, wg_ref[:, cols], preferred_element_type=F32)
        up = jnp.dot(hb, wu_ref[:, cols], preferred_element_type=F32)
        part = _bdot(jax.nn.silu(gate) * up, wd_ref[cols, :])
        ffn = part if ffn is None else ffn + part
    x2 = x1 + ffn
    o_ref[...] = _rms(x2, fin_ref[...]) if final else x2


def _dense_ffn(x1, nf, wg, wu, wd, fin, final):
    r = x1.shape[0]
    tm = min(r, 512)
    row = pl.BlockSpec((tm, D_MODEL), lambda i: (i, 0))
    return pl.pallas_call(
        functools.partial(_dense_ffn_kernel, final=final),
        grid=(r // tm,),
        in_specs=[
            row,
            _resident((1, D_MODEL)),
            _single((D_MODEL, D_FF)),
            _single((D_MODEL, D_FF)),
            _single((D_FF, D_MODEL)),
            _resident((1, D_MODEL)),
        ],
        out_specs=row,
        out_shape=jax.ShapeDtypeStruct((r, D_MODEL), F32),
        compiler_params=_params(("parallel",)),
        name="dense_ffn",
    )(x1, nf, wg, wu, wd, fin)


def _moe_kernel(x_ref, nf_ref, rw_ref, rb_ref, wg_ref, wu_ref, wd_ref, fin_ref, o_ref,
                hb_scr, comb_scr, acc_scr, *, final):
    e = pl.program_id(1)
    lane = lax.broadcasted_iota(jnp.int32, comb_scr.shape, 1)

    @pl.when(e == 0)
    def _():
        hn = _rms(x_ref[...], nf_ref[...])
        hb_scr[...] = hn.astype(BF16)
        logits = jnp.dot(hn, rw_ref[...], preferred_element_type=F32,
                         precision=lax.Precision.HIGHEST) + rb_ref[...]
        neg = jnp.float32(-jnp.inf)
        lg = jnp.where(lane < N_EXPERTS, logits, neg)
        m1 = jnp.max(lg, axis=-1, keepdims=True)
        i1 = jnp.min(jnp.where(lg == m1, lane, LANE_TILE), axis=-1, keepdims=True)
        lg2 = jnp.where(lane == i1, neg, lg)
        m2 = jnp.max(lg2, axis=-1, keepdims=True)
        i2 = jnp.min(jnp.where(lg2 == m2, lane, LANE_TILE), axis=-1, keepdims=True)
        e2 = jnp.exp(m2 - m1)
        den = 1.0 + e2
        comb_scr[...] = jnp.where(lane == i1, 1.0 / den, 0.0) + jnp.where(lane == i2, e2 / den, 0.0)
        acc_scr[...] = jnp.zeros_like(acc_scr)

    hb = hb_scr[...]
    gate = jnp.dot(hb, wg_ref[...], preferred_element_type=F32)
    up = jnp.dot(hb, wu_ref[...], preferred_element_type=F32)
    out_e = _bdot(jax.nn.silu(gate) * up, wd_ref[...])
    w_e = jnp.sum(jnp.where(lane == e, comb_scr[...], 0.0), axis=-1, keepdims=True)
    acc_scr[...] += w_e * out_e

    @pl.when(e == N_EXPERTS - 1)
    def _():
        x2 = x_ref[...] + acc_scr[...]
        o_ref[...] = _rms(x2, fin_ref[...]) if final else x2


def _moe_ffn(x1, nf, rw, rb, wg, wu, wd, fin, final):
    r = x1.shape[0]
    tm = min(r, 512)
    row = pl.BlockSpec((tm, D_MODEL), lambda i, e: (i, 0))
    return pl.pallas_call(
        functools.partial(_moe_kernel, final=final),
        grid=(r // tm, N_EXPERTS),
        in_specs=[
            row,
            _resident((1, D_MODEL)),
            _single((D_MODEL, LANE_TILE)),
            _resident((1, LANE_TILE)),
            pl.BlockSpec((None, D_MODEL, D_EXP), lambda i, e: (e, 0, 0)),
            pl.BlockSpec((None, D_MODEL, D_EXP), lambda i, e: (e, 0, 0)),
            pl.BlockSpec((None, D_EXP, D_MODEL), lambda i, e: (e, 0, 0)),
            _resident((1, D_MODEL)),
        ],
        out_specs=row,
        out_shape=jax.ShapeDtypeStruct((r, D_MODEL), F32),
        scratch_shapes=[
            pltpu.VMEM((tm, D_MODEL), BF16),
            pltpu.VMEM((tm, LANE_TILE), F32),
            pltpu.VMEM((tm, D_MODEL), F32),
        ],
        compiler_params=_params(("parallel", "arbitrary")),
        name="moe_ffn",
    )(x1, nf, rw, rb, wg, wu, wd, fin)


def _block_diag(w):
    h, i, j = w.shape
    eye = jnp.eye(h, dtype=w.dtype)
    return jnp.einsum("hij,hk->hikj", w, eye).reshape(h * i, h * j)


def _s5_in_matrix(bbr, bbi):
    gh = G_C // S5_HALVES
    out = []
    for part in (bbr, bbi):
        p4 = part.reshape(S5_HALVES, gh, HG_C, P_STATE)
        eye = jnp.eye(gh, dtype=part.dtype)
        out.append(jnp.einsum("aghp,gk->aghkp", p4, eye).reshape(S5_HALVES, gh * HG_C, gh * P_STATE))
    return jnp.concatenate(out, axis=-1).astype(BF16)


def _s5_out_matrix(c_re, c_im):
    gh = G_C // S5_HALVES
    out = []
    for part in (c_re, -c_im):
        p4 = part.reshape(S5_HALVES, gh, HG_C, P_STATE)
        eye = jnp.eye(gh, dtype=part.dtype)
        out.append(jnp.einsum("agop,gk->agpko", p4, eye).reshape(S5_HALVES, gh * P_STATE, gh * HG_C))
    return jnp.concatenate(out, axis=1).astype(BF16)


def _s5_state_in(re, im):
    n = re.shape[0]
    parts = [a.reshape(n, S5_HALVES, 1, S5_HALF_RI) for a in (re, im)]
    return jnp.concatenate(parts, axis=2).reshape(n, S5_LANES)


def _s5_state_out(s):
    n = s.shape[0]
    s4 = s.reshape(n, S5_HALVES, 2, S5_HALF_RI)
    return s4[:, :, 0].reshape(n, G_C, P_STATE), s4[:, :, 1].reshape(n, G_C, P_STATE)


def _lane_vec(v):
    return v.reshape(1, -1)


def _layer_weights(l, w):
    ar, ai, bbr, bbi = _s5_discretize(w["c_lam_re"][l], w["c_lam_im"][l], w["c_log_dt"][l],
                                      w["c_b_re"][l], w["c_b_im"][l])
    causal = jnp.tril(jnp.ones((CHUNK, CHUNK), dtype=bool))
    ws = w["a_ws"][l]
    lw = dict(
        norm_mix_g=_lane_vec(w["norm_mix_g"][l]),
        w_in=w["w_in"][l].astype(BF16),
        b_gate=_lane_vec(w["b_gate"][l]),
        a_ln_g=_lane_vec(w["a_ln_g"][l]),
        a_ln_b=_lane_vec(w["a_ln_b"][l]),
        a_wm=jnp.where(causal[None], ws, jnp.zeros_like(ws)),
        a_ws=ws,
        a_bs=w["a_bs"][l],
        b_cw=w["b_conv_w"][l][:, None, :],
        b_cb=_lane_vec(w["b_conv_b"][l]),
        b_wax=jnp.concatenate([_block_diag(w["b_wa"][l]), _block_diag(w["b_wx"][l])], axis=1).astype(BF16),
        b_ba=_lane_vec(w["b_ba"][l]),
        b_bx=_lane_vec(w["b_bx"][l]),
        b_lam=_lane_vec(w["b_lam"][l]),
        c_ar=ar.reshape(S5_HALVES, S5_HALF_RI).reshape(1, S5_LANES // 2),
        c_ai=ai.reshape(S5_HALVES, S5_HALF_RI).reshape(1, S5_LANES // 2),
        c_bd=_s5_in_matrix(bbr, bbi),
        c_cd=_s5_out_matrix(w["c_c_re"][l], w["c_c_im"][l]),
        c_d=_lane_vec(w["c_d"][l]),
        c_glu_w=w["c_glu_w"][l].astype(BF16),
        c_glu_b=_lane_vec(w["c_glu_b"][l]),
        w_branch=w["w_branch"][l].astype(BF16),
        w_out=w["w_out"][l].astype(BF16),
        norm_ffn_g=_lane_vec(w["norm_ffn_g"][l]),
        final_norm_g=_lane_vec(w["final_norm_g"]),
    )
    j = l // 2
    if l % 2 == 0:
        lw.update(ffn_wg=w["ffn_w_gate"][j].astype(BF16), ffn_wu=w["ffn_w_up"][j].astype(BF16),
                  ffn_wd=w["ffn_w_down"][j].astype(BF16))
    else:
        pad = LANE_TILE - N_EXPERTS
        lw.update(moe_rw=jnp.pad(w["moe_router_w"][j], ((0, 0), (0, pad))),
                  moe_rb=jnp.pad(w["moe_router_b"][j], (0, pad)).reshape(1, LANE_TILE),
                  moe_wg=w["moe_w_gate"][j].astype(BF16), moe_wu=w["moe_w_up"][j].astype(BF16),
                  moe_wd=w["moe_w_down"][j].astype(BF16))
    return lw


def _trunk(x_tm, t, n, conv_s, lru_s, s5re_s, s5im_s, layers):
    x = x_tm
    vas, convs, lrus, sres, sims = [], [], [], [], []
    for l, lw in enumerate(layers):
        proj, gates = _in_proj(x, lw["norm_mix_g"], lw["w_in"], lw["b_gate"], lw["a_ln_g"], lw["a_ln_b"])
        if t % CHUNK == 0:
            eye = jnp.eye(n, dtype=F32)
            wk = jnp.einsum("hts,nm->htnsm", lw["a_wm"], eye).reshape(H_A, CHUNK * n, CHUNK * n)
            bk = jnp.broadcast_to(lw["a_bs"].T[:, None, :, None], (CHUNK, n, H_A, HD_A))
            ya = _mixer_a_prompt(proj, t, n, wk.astype(BF16), bk.reshape(CHUNK * n, D_A))
        else:
            assert t < CHUNK
            wv = jnp.repeat(jnp.transpose(lw["a_ws"][:, :t, :t], (1, 2, 0)), HD_A, axis=-1)[:, :, None, :]
            bv = jnp.repeat(jnp.transpose(lw["a_bs"][:, :t], (1, 0)), HD_A, axis=-1)[:, None, :]
            ya = _mixer_a_sample(proj, t, n, wv, bv)
        yb, conv_new, h_last = _mixer_b(proj, t, n, jnp.swapaxes(conv_s[l], 0, 1), lru_s[l], lw["b_cw"],
                                        lw["b_cb"], lw["b_wax"], lw["b_ba"], lw["b_bx"], lw["b_lam"])
        yc, s_last = _mixer_c(proj, t, n, _s5_state_in(s5re_s[l], s5im_s[l]), lw["c_ar"], lw["c_ai"],
                              lw["c_bd"], lw["c_cd"], lw["c_d"], lw["c_glu_w"], lw["c_glu_b"])
        final = l == len(layers) - 1
        x1 = _merge(ya, yb, yc, gates, x, lw["w_branch"], lw["w_out"])
        if l % 2 == 0:
            x = _dense_ffn(x1, lw["norm_ffn_g"], lw["ffn_wg"], lw["ffn_wu"], lw["ffn_wd"],
                           lw["final_norm_g"], final)
        else:
            x = _moe_ffn(x1, lw["norm_ffn_g"], lw["moe_rw"], lw["moe_rb"], lw["moe_wg"], lw["moe_wu"],
                         lw["moe_wd"], lw["final_norm_g"], final)
        va = proj.reshape(t, n, PA_COLS)[:, :, BLK_VA * COL_BLK:(BLK_VA + 1) * COL_BLK]
        vas.append(jnp.swapaxes(va, 0, 1))
        convs.append(jnp.swapaxes(conv_new, 0, 1))
        lrus.append(h_last)
        s_re, s_im = _s5_state_out(s_last)
        sres.append(s_re)
        sims.append(s_im)
    return x, jnp.stack(vas), jnp.stack(convs), jnp.stack(lrus), jnp.stack(sres), jnp.stack(sims)


def kernel(x_prompt, x_sample, state_conv_b, state_lru_b, state_s5_re, state_s5_im, norm_mix_g, w_in, b_gate, a_ln_g, a_ln_b, a_ws, a_bs, b_conv_w, b_conv_b, b_wa, b_ba, b_wx, b_bx, b_lam, c_lam_re, c_lam_im, c_log_dt, c_b_re, c_b_im, c_c_re, c_c_im, c_d, c_glu_w, c_glu_b, w_branch, w_out, norm_ffn_g, ffn_w_gate, ffn_w_up, ffn_w_down, moe_router_w, moe_router_b, moe_w_gate, moe_w_up, moe_w_down, final_norm_g):
    w = dict(norm_mix_g=norm_mix_g, w_in=w_in, b_gate=b_gate, a_ln_g=a_ln_g, a_ln_b=a_ln_b, a_ws=a_ws,
             a_bs=a_bs, b_conv_w=b_conv_w, b_conv_b=b_conv_b, b_wa=b_wa, b_ba=b_ba, b_wx=b_wx, b_bx=b_bx,
             b_lam=b_lam, c_lam_re=c_lam_re, c_lam_im=c_lam_im, c_log_dt=c_log_dt, c_b_re=c_b_re,
             c_b_im=c_b_im, c_c_re=c_c_re, c_c_im=c_c_im, c_d=c_d, c_glu_w=c_glu_w, c_glu_b=c_glu_b,
             w_branch=w_branch, w_out=w_out, norm_ffn_g=norm_ffn_g, ffn_w_gate=ffn_w_gate,
             ffn_w_up=ffn_w_up, ffn_w_down=ffn_w_down, moe_router_w=moe_router_w,
             moe_router_b=moe_router_b, moe_w_gate=moe_w_gate, moe_w_up=moe_w_up, moe_w_down=moe_w_down,
             final_norm_g=final_norm_g)
    layers = [_layer_weights(l, w) for l in range(DEPTH)]

    nb, tp, _ = x_prompt.shape
    ns, ts, _ = x_sample.shape
    conv0 = jnp.zeros((DEPTH, nb, CONV_W - 1, D_B), F32)
    lru0 = jnp.zeros((DEPTH, nb, D_B), F32)
    s50 = jnp.zeros((DEPTH, nb, G_C, P_STATE), F32)
    xp = jnp.swapaxes(x_prompt, 0, 1).reshape(tp * nb, D_MODEL)
    yp, _, conv_p, lru_p, sre_p, sim_p = _trunk(xp, tp, nb, conv0, lru0, s50, s50, layers)
    xs = jnp.swapaxes(x_sample, 0, 1).reshape(ts * ns, D_MODEL)
    ys, va_s, conv_s, lru_s, sre_s, sim_s = _trunk(xs, ts, ns, state_conv_b, state_lru_b, state_s5_re,
                                                   state_s5_im, layers)
    y_prompt = jnp.swapaxes(yp.reshape(tp, nb, D_MODEL), 0, 1)
    y_sample = jnp.swapaxes(ys.reshape(ts, ns, D_MODEL), 0, 1)
    return (y_prompt, y_sample, conv_p, lru_p, sre_p, sim_p, va_s, conv_s, lru_s, sre_s, sim_s)
```

```python
import functools
import math

import jax
import jax.numpy as jnp
from jax import lax
from jax.experimental import pallas as pl
from jax.experimental.pallas import tpu as pltpu

D_MODEL = 1024
DEPTH = 2
CHUNK = 128
D_A = D_MODEL // 2
H_A = 4
HD_A = D_A // H_A
D_B = D_MODEL // 2
H_B = 8
BH_B = D_B // H_B
CONV_W = 4
LRU_C = 8.0
D_C = D_MODEL // 2
HG_C = 16
G_C = D_C // HG_C
P_STATE = 64
N_BRANCH = 3
D_BR = D_MODEL // 2
IN_COLS = 2 * D_A + 2 * D_B + D_C + N_BRANCH * D_MODEL
D_FF = 11 * D_MODEL // 4
N_EXPERTS = 8
TOP_K = 2
D_EXP = D_FF // 2
EPS = 1e-6

F32 = jnp.float32
BF16 = jnp.bfloat16

COL_BLK = 512
N_COL_BLK = IN_COLS // COL_BLK
BLK_UA, BLK_VA, BLK_XB, BLK_GB, BLK_UC, BLK_GATE = 0, 1, 2, 3, 4, 5
PA_COLS = BLK_GATE * COL_BLK
S5_HALVES = 2
S5_HALF_IN = D_C // S5_HALVES
S5_HALF_RI = (G_C // S5_HALVES) * P_STATE
S5_HALF_ST = 2 * S5_HALF_RI
S5_LANES = S5_HALVES * S5_HALF_ST
LANE_TILE = 128
SUBLANE_TILE = 8
MXU_DIM = 256
VMEM_LIMIT = 56 * 1024 * 1024


def _resident(shape):
    zeros = (0,) * len(shape)
    return pl.BlockSpec(shape, lambda *_: zeros)


def _params(sem):
    return pltpu.CompilerParams(dimension_semantics=sem, vmem_limit_bytes=VMEM_LIMIT)


def _rms(x, g):
    return x * lax.rsqrt(jnp.mean(x * x, axis=-1, keepdims=True) + EPS) * g


def _bdot(a, b):
    return jnp.dot(a.astype(BF16), b, preferred_element_type=F32)


def _in_proj_kernel(x_ref, g_ref, w_ref, bg_ref, lng_ref, lnb_ref, pa_ref, gate_ref):
    hb = _rms(x_ref[...], g_ref[...]).astype(BF16)
    for j in range(N_COL_BLK):
        cols = slice(j * COL_BLK, (j + 1) * COL_BLK)
        acc = jnp.dot(hb, w_ref[:, cols], preferred_element_type=F32)
        if j in (BLK_UA, BLK_GB):
            pa_ref[:, cols] = jax.nn.gelu(acc)
        elif j == BLK_VA:
            v = jax.nn.gelu(acc)
            vc = v - jnp.mean(v, axis=-1, keepdims=True)
            var = jnp.mean(vc * vc, axis=-1, keepdims=True)
            pa_ref[:, cols] = vc * lax.rsqrt(var + EPS) * lng_ref[...] + lnb_ref[...]
        elif j in (BLK_XB, BLK_UC):
            pa_ref[:, cols] = acc
        else:
            gcols = slice(cols.start - PA_COLS, cols.stop - PA_COLS)
            gate_ref[:, gcols] = jax.nn.sigmoid(acc + bg_ref[:, gcols]).astype(BF16)


def _in_proj(x, norm_g, w_in, b_gate, ln_g, ln_b):
    r = x.shape[0]
    tm = min(r, 512)
    return pl.pallas_call(
        _in_proj_kernel,
        grid=(r // tm,),
        in_specs=[
            pl.BlockSpec((tm, D_MODEL), lambda i: (i, 0)),
            _resident((1, D_MODEL)),
            _single((D_MODEL, IN_COLS)),
            _resident((1, N_BRANCH * D_MODEL)),
            _resident((1, D_A)),
            _resident((1, D_A)),
        ],
        out_specs=[
            pl.BlockSpec((tm, PA_COLS), lambda i: (i, 0)),
            pl.BlockSpec((tm, N_BRANCH * D_MODEL), lambda i: (i, 0)),
        ],
        out_shape=[
            jax.ShapeDtypeStruct((r, PA_COLS), F32),
            jax.ShapeDtypeStruct((r, N_BRANCH * D_MODEL), BF16),
        ],
        compiler_params=_params(("parallel",)),
        name="in_proj",
    )(x, norm_g, w_in, b_gate, ln_g, ln_b)


def _mixer_a_mxu_kernel(u_ref, v_ref, wm_ref, bk_ref, o_ref, wk_scr, *, rows, n):
    @pl.when(pl.program_id(0) == 0)
    def _():
        shift = n.bit_length() - 1
        e_rows = (lax.broadcasted_iota(jnp.int32, (rows, CHUNK), 0) >> shift
                  == lax.broadcasted_iota(jnp.int32, (rows, CHUNK), 1)).astype(BF16)
        e_cols = (lax.broadcasted_iota(jnp.int32, (CHUNK, rows), 0)
                  == lax.broadcasted_iota(jnp.int32, (CHUNK, rows), 1) >> shift).astype(BF16)
        same_seq = ((lax.broadcasted_iota(jnp.int32, (rows, rows), 0) & (n - 1))
                    == (lax.broadcasted_iota(jnp.int32, (rows, rows), 1) & (n - 1)))
        for h in range(H_A):
            left = jnp.dot(e_rows, wm_ref[h], preferred_element_type=F32).astype(BF16)
            full = jnp.dot(left, e_cols, preferred_element_type=F32)
            wk_scr[h] = jnp.where(same_seq, full, 0.0).astype(BF16)

    vb = v_ref[...].astype(BF16)
    for h in range(H_A):
        cols = slice(h * HD_A, (h + 1) * HD_A)
        for r0 in range(0, rows, MXU_DIM):
            k = r0 + MXU_DIM
            mixed = jnp.dot(wk_scr[h, r0:k, 0:k], vb[0:k, cols], preferred_element_type=F32)
            o_ref[r0:k, cols] = u_ref[r0:k, cols] * (mixed + bk_ref[r0:k, cols])


def _mixer_a_prompt(pa, t, n, wm, bk):
    rows = CHUNK * n
    assert rows % MXU_DIM == 0 and MXU_DIM % n == 0 and n & (n - 1) == 0
    return pl.pallas_call(
        functools.partial(_mixer_a_mxu_kernel, rows=rows, n=n),
        grid=(t // CHUNK,),
        in_specs=[
            pl.BlockSpec((rows, COL_BLK), lambda c: (c, BLK_UA)),
            pl.BlockSpec((rows, COL_BLK), lambda c: (c, BLK_VA)),
            _single((H_A, CHUNK, CHUNK)),
            _single((rows, D_A)),
        ],
        out_specs=pl.BlockSpec((rows, D_A), lambda c: (c, 0)),
        out_shape=jax.ShapeDtypeStruct((t * n, D_A), F32),
        scratch_shapes=[pltpu.VMEM((H_A, rows, rows), BF16)],
        compiler_params=_params(("arbitrary",)),
        name="mixer_a_prompt",
    )(pa, pa, wm, bk)


def _mixer_a_vpu_kernel(u_ref, v_ref, wv_ref, bv_ref, o_ref, *, t):
    for i in range(t):
        mixed = wv_ref[i, 0] * v_ref[0]
        for s in range(1, i + 1):
            mixed = mixed + wv_ref[i, s] * v_ref[s]
        o_ref[i] = u_ref[i] * (mixed + bv_ref[i])


def _mixer_a_sample(proj, t, n, wv, bv):
    proj3 = proj.reshape(t, n, PA_COLS)
    out = pl.pallas_call(
        functools.partial(_mixer_a_vpu_kernel, t=t),
        grid=(1,),
        in_specs=[
            pl.BlockSpec((t, n, COL_BLK), lambda i: (0, 0, BLK_UA)),
            pl.BlockSpec((t, n, COL_BLK), lambda i: (0, 0, BLK_VA)),
            _resident((t, t, 1, D_A)),
            _resident((t, 1, D_A)),
        ],
        out_specs=_resident((t, n, D_A)),
        out_shape=jax.ShapeDtypeStruct((t, n, D_A), F32),
        compiler_params=_params(("arbitrary",)),
        name="mixer_a_sample",
    )(proj3, proj3, wv, bv)
    return out.reshape(t * n, D_A)


def _lru_kernel(xb_ref, gb_ref, conv0_ref, h0_ref, cw_ref, cb_ref, wax_ref, ba_ref, bx_ref, lam_ref,
                yb_ref, convn_ref, hlast_ref, xp_scr, a_scr, d_scr, h_scr, *, tt, n):
    @pl.when(pl.program_id(0) == 0)
    def _():
        xp_scr[0:CONV_W - 1] = conv0_ref[...]
        h_scr[...] = h0_ref[...]

    xp_scr[CONV_W - 1:CONV_W - 1 + tt] = xb_ref[...]
    xc = cb_ref[...] + xp_scr[0:tt] * cw_ref[0]
    for k in range(1, CONV_W):
        xc = xc + xp_scr[k:k + tt] * cw_ref[k]
    tail = xp_scr[tt:tt + CONV_W - 1]
    convn_ref[...] = tail
    xp_scr[0:CONV_W - 1] = tail

    xc2 = xc.reshape(tt * n, D_B)
    pre = _bdot(xc2, wax_ref[...])
    r = jax.nn.sigmoid(pre[:, :D_B] + ba_ref[...])
    i = jax.nn.sigmoid(pre[:, D_B:] + bx_ref[...])
    log_a = -LRU_C * r * jax.nn.softplus(-lam_ref[...])
    a_scr[...] = jnp.exp(log_a).reshape(tt, n, D_B)
    gain = jnp.sqrt(-jnp.tanh(log_a) * (jnp.exp(2.0 * log_a) + 1.0))
    d_scr[...] = (gain * (i * xc2)).reshape(tt, n, D_B)

    def step(s, h):
        h = a_scr[s] * h + d_scr[s]
        d_scr[s] = h
        return h

    h = lax.fori_loop(0, tt, step, h_scr[...], unroll=min(tt, 8))
    h_scr[...] = h
    hlast_ref[...] = h
    yb_ref[...] = d_scr[...] * gb_ref[...]


def _mixer_b(proj, t, n, conv0, h0, cw, cb, wax, ba, bx, lam):
    tt = min(t, 512 // n) if n <= 512 else 1
    proj3 = proj.reshape(t, n, PA_COLS)
    yb, conv_new, h_last = pl.pallas_call(
        functools.partial(_lru_kernel, tt=tt, n=n),
        grid=(t // tt,),
        in_specs=[
            pl.BlockSpec((tt, n, COL_BLK), lambda i: (i, 0, BLK_XB)),
            pl.BlockSpec((tt, n, COL_BLK), lambda i: (i, 0, BLK_GB)),
            _resident((CONV_W - 1, n, D_B)),
            _resident((n, D_B)),
            _resident((CONV_W, 1, D_B)),
            _resident((1, D_B)),
            _resident((D_B, 2 * D_B)),
            _resident((1, D_B)),
            _resident((1, D_B)),
            _resident((1, D_B)),
        ],
        out_specs=[
            pl.BlockSpec((tt, n, D_B), lambda i: (i, 0, 0)),
            _resident((CONV_W - 1, n, D_B)),
            _resident((n, D_B)),
        ],
        out_shape=[
            jax.ShapeDtypeStruct((t, n, D_B), F32),
            jax.ShapeDtypeStruct((CONV_W - 1, n, D_B), F32),
            jax.ShapeDtypeStruct((n, D_B), F32),
        ],
        scratch_shapes=[
            pltpu.VMEM((tt + CONV_W - 1, n, D_B), F32),
            pltpu.VMEM((tt, n, D_B), F32),
            pltpu.VMEM((tt, n, D_B), F32),
            pltpu.VMEM((n, D_B), F32),
        ],
        compiler_params=_params(("arbitrary",)),
        name="mixer_b",
    )(proj3, proj3, conv0, h0, cw, cb, wax, ba, bx, lam)
    return yb.reshape(t * n, D_B), conv_new, h_last


def _s5_disc_kernel(lr_ref, li_ref, ldt_ref, bre_ref, bim_ref, ar_ref, ai_ref, bbr_ref, bbi_ref):
    lr = lr_ref[...]
    li = li_ref[...]
    dt = jnp.exp(ldt_ref[...])
    mag = jnp.exp(lr * dt)
    ar = mag * jnp.cos(li * dt)
    ai = mag * jnp.sin(li * dt)
    ar_ref[...] = ar
    ai_ref[...] = ai
    den = lr * lr + li * li
    qr = ((ar - 1.0) * lr + ai * li) / den
    qi = (ai * lr - (ar - 1.0) * li) / den
    bre = bre_ref[...]
    bim = bim_ref[...]
    bbr_ref[...] = qr * bre - qi * bim
    bbi_ref[...] = qr * bim + qi * bre


def _s5_discretize(lam_re, lam_im, log_dt, b_re, b_im):
    g3 = jax.ShapeDtypeStruct((G_C, 1, P_STATE), F32)
    b3 = jax.ShapeDtypeStruct((G_C, HG_C, P_STATE), F32)
    return pl.pallas_call(
        _s5_disc_kernel,
        out_shape=[g3, g3, b3, b3],
        name="s5_discretize",
    )(lam_re.reshape(G_C, 1, P_STATE), lam_im.reshape(G_C, 1, P_STATE), log_dt.reshape(G_C, 1, 1),
      jnp.swapaxes(b_re, 1, 2), jnp.swapaxes(b_im, 1, 2))


def _s5_kernel(u_ref, s0_ref, ar_ref, ai_ref, bd_ref, cd_ref, d_ref, gw_ref, gb_ref,
               yc_ref, slast_ref, bu_scr, st_scr, *, tt, n):
    @pl.when(pl.program_id(0) == 0)
    def _():
        st_scr[...] = s0_ref[...]

    u2 = u_ref[...].reshape(tt * n, D_C)
    ub = u2.astype(BF16)
    for hf in range(S5_HALVES):
        bu = jnp.dot(ub[:, hf * S5_HALF_IN:(hf + 1) * S5_HALF_IN], bd_ref[hf], preferred_element_type=F32)
        bu_scr[:, :, hf * S5_HALF_ST:(hf + 1) * S5_HALF_ST] = bu.reshape(tt, n, S5_HALF_ST)

    lane_blk = 4 * LANE_TILE
    for hf in range(S5_HALVES):
        for q in range(S5_HALF_RI // lane_blk):
            lr = hf * S5_HALF_ST + q * lane_blk
            li = lr + S5_HALF_RI
            la = hf * S5_HALF_RI + q * lane_blk
            ar = jnp.broadcast_to(ar_ref[:, la:la + lane_blk], (SUBLANE_TILE, lane_blk))
            ai = jnp.broadcast_to(ai_ref[:, la:la + lane_blk], (SUBLANE_TILE, lane_blk))

            def rows_body(rg, carry, lr=lr, li=li, ar=ar, ai=ai):
                rs = pl.ds(pl.multiple_of(rg * SUBLANE_TILE, SUBLANE_TILE), SUBLANE_TILE)

                def step(s, x):
                    xr, xi = x
                    nr = ar * xr - ai * xi + bu_scr[s, rs, lr:lr + lane_blk]
                    ni = ar * xi + ai * xr + bu_scr[s, rs, li:li + lane_blk]
                    bu_scr[s, rs, lr:lr + lane_blk] = nr
                    bu_scr[s, rs, li:li + lane_blk] = ni
                    return nr, ni

                x0 = (st_scr[rs, lr:lr + lane_blk], st_scr[rs, li:li + lane_blk])
                xr, xi = lax.fori_loop(0, tt, step, x0, unroll=min(tt, 8))
                st_scr[rs, lr:lr + lane_blk] = xr
                st_scr[rs, li:li + lane_blk] = xi
                return carry

            lax.fori_loop(0, n // SUBLANE_TILE, rows_body, 0)

    slast_ref[...] = st_scr[...]
    ys = []
    for hf in range(S5_HALVES):
        xs = bu_scr[:, :, hf * S5_HALF_ST:(hf + 1) * S5_HALF_ST].reshape(tt * n, S5_HALF_ST)
        ys.append(_bdot(xs, cd_ref[hf]))
    y = jnp.concatenate(ys, axis=-1) + d_ref[...] * u2
    z = jax.nn.gelu(y)
    yc = z * jax.nn.sigmoid(_bdot(z, gw_ref[...]) + gb_ref[...])
    yc_ref[...] = yc.reshape(tt, n, D_C)


def _mixer_c(proj, t, n, s0, ar, ai, bd, cd, d, glu_w, glu_b):
    tt = min(t, 512 // n) if n <= 512 else 1
    proj3 = proj.reshape(t, n, PA_COLS)
    yc, s_last = pl.pallas_call(
        functools.partial(_s5_kernel, tt=tt, n=n),
        grid=(t // tt,),
        in_specs=[
            pl.BlockSpec((tt, n, COL_BLK), lambda i: (i, 0, BLK_UC)),
            _resident((n, S5_LANES)),
            _resident((1, S5_LANES // 2)),
            _resident((1, S5_LANES // 2)),
            _resident((S5_HALVES, S5_HALF_IN, S5_HALF_ST)),
            _resident((S5_HALVES, S5_HALF_ST, S5_HALF_IN)),
            _resident((1, D_C)),
            _resident((D_C, D_C)),
            _resident((1, D_C)),
        ],
        out_specs=[
            pl.BlockSpec((tt, n, D_C), lambda i: (i, 0, 0)),
            _resident((n, S5_LANES)),
        ],
        out_shape=[
            jax.ShapeDtypeStruct((t, n, D_C), F32),
            jax.ShapeDtypeStruct((n, S5_LANES), F32),
        ],
        scratch_shapes=[
            pltpu.VMEM((tt, n, S5_LANES), F32),
            pltpu.VMEM((n, S5_LANES), F32),
        ],
        compiler_params=_params(("arbitrary",)),
        name="mixer_c",
    )(proj3, s0, ar, ai, bd, cd, d, glu_w, glu_b)
    return yc.reshape(t * n, D_C), s_last


def _single(shape):
    zeros = (0,) * len(shape)
    return pl.BlockSpec(shape, lambda *_: zeros, pipeline_mode=pl.Buffered(1))


def _merge_kernel(ya_ref, yb_ref, yc_ref, g_ref, x_ref, wbr_ref, wout_ref, o_ref):
    m = None
    for k, y_ref in enumerate((ya_ref, yb_ref, yc_ref)):
        p = _bdot(y_ref[...], wbr_ref[k])
        g = g_ref[:, k * D_MODEL:(k + 1) * D_MODEL].astype(F32)
        m = g * p if m is None else m + g * p
    o_ref[...] = x_ref[...] + _bdot(m, wout_ref[...])


def _merge(ya, yb, yc, gates, x, wbr, wout):
    r = x.shape[0]
    tm = min(r, 512)
    row = pl.BlockSpec((tm, D_MODEL), lambda i: (i, 0))
    return pl.pallas_call(
        _merge_kernel,
        grid=(r // tm,),
        in_specs=[pl.BlockSpec((tm, D_BR), lambda i: (i, 0))] * N_BRANCH + [
            pl.BlockSpec((tm, N_BRANCH * D_MODEL), lambda i: (i, 0)),
            row,
            _single((N_BRANCH, D_BR, D_MODEL)),
            _single((D_MODEL, D_MODEL)),
        ],
        out_specs=row,
        out_shape=jax.ShapeDtypeStruct((r, D_MODEL), F32),
        compiler_params=_params(("parallel",)),
        name="merge",
    )(ya, yb, yc, gates, x, wbr, wout)


def _dense_ffn_kernel(x_ref, nf_ref, wg_ref, wu_ref, wd_ref, fin_ref, o_ref, *, final):
    x1 = x_ref[...]
    hb = _rms(x1, nf_ref[...]).astype(BF16)
    ffn = None
    for c in range(D_FF // D_EXP):
        cols = slice(c * D_EXP, (c + 1) * D_EXP)
        gate = jnp.dot(hb, wg_ref[:, cols], preferred_element_type=F32)
        up = jnp.dot(hb, wu_ref[:, cols], preferred_element_type=F32)
        part = _bdot(jax.nn.silu(gate) * up, wd_ref[cols, :])
        ffn = part if ffn is None else ffn + part
    x2 = x1 + ffn
    o_ref[...] = _rms(x2, fin_ref[...]) if final else x2


def _dense_ffn(x1, nf, wg, wu, wd, fin, final):
    r = x1.shape[0]
    tm = min(r, 512)
    row = pl.BlockSpec((tm, D_MODEL), lambda i: (i, 0))
    return pl.pallas_call(
        functools.partial(_dense_ffn_kernel, final=final),
        grid=(r // tm,),
        in_specs=[
            row,
            _resident((1, D_MODEL)),
            _single((D_MODEL, D_FF)),
            _single((D_MODEL, D_FF)),
            _single((D_FF, D_MODEL)),
            _resident((1, D_MODEL)),
        ],
        out_specs=row,
        out_shape=jax.ShapeDtypeStruct((r, D_MODEL), F32),
        compiler_params=_params(("parallel",)),
        name="dense_ffn",
    )(x1, nf, wg, wu, wd, fin)


def _moe_kernel(x_ref, nf_ref, rw_ref, rb_ref, wg_ref, wu_ref, wd_ref, fin_ref, o_ref,
                hb_scr, comb_scr, acc_scr, *, final):
    e = pl.program_id(1)
    lane = lax.broadcasted_iota(jnp.int32, comb_scr.shape, 1)

    @pl.when(e == 0)
    def _():
        hn = _rms(x_ref[...], nf_ref[...])
        hb_scr[...] = hn.astype(BF16)
        logits = jnp.dot(hn, rw_ref[...], preferred_element_type=F32,
                         precision=lax.Precision.HIGHEST) + rb_ref[...]
        neg = jnp.float32(-jnp.inf)
        lg = jnp.where(lane < N_EXPERTS, logits, neg)
        m1 = jnp.max(lg, axis=-1, keepdims=True)
        i1 = jnp.min(jnp.where(lg == m1, lane, LANE_TILE), axis=-1, keepdims=True)
        lg2 = jnp.where(lane == i1, neg, lg)
        m2 = jnp.max(lg2, axis=-1, keepdims=True)
        i2 = jnp.min(jnp.where(lg2 == m2, lane, LANE_TILE), axis=-1, keepdims=True)
        e2 = jnp.exp(m2 - m1)
        den = 1.0 + e2
        comb_scr[...] = jnp.where(lane == i1, 1.0 / den, 0.0) + jnp.where(lane == i2, e2 / den, 0.0)
        acc_scr[...] = jnp.zeros_like(acc_scr)

    hb = hb_scr[...]
    gate = jnp.dot(hb, wg_ref[...], preferred_element_type=F32)
    up = jnp.dot(hb, wu_ref[...], preferred_element_type=F32)
    out_e = _bdot(jax.nn.silu(gate) * up, wd_ref[...])
    w_e = jnp.sum(jnp.where(lane == e, comb_scr[...], 0.0), axis=-1, keepdims=True)
    acc_scr[...] += w_e * out_e

    @pl.when(e == N_EXPERTS - 1)
    def _():
        x2 = x_ref[...] + acc_scr[...]
        o_ref[...] = _rms(x2, fin_ref[...]) if final else x2


def _moe_ffn(x1, nf, rw, rb, wg, wu, wd, fin, final):
    r = x1.shape[0]
    tm = min(r, 512)
    row = pl.BlockSpec((tm, D_MODEL), lambda i, e: (i, 0))
    return pl.pallas_call(
        functools.partial(_moe_kernel, final=final),
        grid=(r // tm, N_EXPERTS),
        in_specs=[
            row,
            _resident((1, D_MODEL)),
            _single((D_MODEL, LANE_TILE)),
            _resident((1, LANE_TILE)),
            pl.BlockSpec((None, D_MODEL, D_EXP), lambda i, e: (e, 0, 0)),
            pl.BlockSpec((None, D_MODEL, D_EXP), lambda i, e: (e, 0, 0)),
            pl.BlockSpec((None, D_EXP, D_MODEL), lambda i, e: (e, 0, 0)),
            _resident((1, D_MODEL)),
        ],
        out_specs=row,
        out_shape=jax.ShapeDtypeStruct((r, D_MODEL), F32),
        scratch_shapes=[
            pltpu.VMEM((tm, D_MODEL), BF16),
            pltpu.VMEM((tm, LANE_TILE), F32),
            pltpu.VMEM((tm, D_MODEL), F32),
        ],
        compiler_params=_params(("parallel", "arbitrary")),
        name="moe_ffn",
    )(x1, nf, rw, rb, wg, wu, wd, fin)


def _block_diag(w):
    h, i, j = w.shape
    eye = jnp.eye(h, dtype=w.dtype)
    return jnp.einsum("hij,hk->hikj", w, eye).reshape(h * i, h * j)


def _s5_in_matrix(bbr, bbi):
    gh = G_C // S5_HALVES
    out = []
    for part in (bbr, bbi):
        p4 = part.reshape(S5_HALVES, gh, HG_C, P_STATE)
        eye = jnp.eye(gh, dtype=part.dtype)
        out.append(jnp.einsum("aghp,gk->aghkp", p4, eye).reshape(S5_HALVES, gh * HG_C, gh * P_STATE))
    return jnp.concatenate(out, axis=-1).astype(BF16)


def _s5_out_matrix(c_re, c_im):
    gh = G_C // S5_HALVES
    out = []
    for part in (c_re, -c_im):
        p4 = part.reshape(S5_HALVES, gh, HG_C, P_STATE)
        eye = jnp.eye(gh, dtype=part.dtype)
        out.append(jnp.einsum("agop,gk->agpko", p4, eye).reshape(S5_HALVES, gh * P_STATE, gh * HG_C))
    return jnp.concatenate(out, axis=1).astype(BF16)


def _s5_state_in(re, im):
    n = re.shape[0]
    parts = [a.reshape(n, S5_HALVES, 1, S5_HALF_RI) for a in (re, im)]
    return jnp.concatenate(parts, axis=2).reshape(n, S5_LANES)


def _s5_state_out(s):
    n = s.shape[0]
    s4 = s.reshape(n, S5_HALVES, 2, S5_HALF_RI)
    return s4[:, :, 0].reshape(n, G_C, P_STATE), s4[:, :, 1].reshape(n, G_C, P_STATE)


def _lane_vec(v):
    return v.reshape(1, -1)


def _layer_weights(l, w):
    ar, ai, bbr, bbi = _s5_discretize(w["c_lam_re"][l], w["c_lam_im"][l], w["c_log_dt"][l],
                                      w["c_b_re"][l], w["c_b_im"][l])
    causal = jnp.tril(jnp.ones((CHUNK, CHUNK), dtype=bool))
    ws = w["a_ws"][l]
    lw = dict(
        norm_mix_g=_lane_vec(w["norm_mix_g"][l]),
        w_in=w["w_in"][l].astype(BF16),
        b_gate=_lane_vec(w["b_gate"][l]),
        a_ln_g=_lane_vec(w["a_ln_g"][l]),
        a_ln_b=_lane_vec(w["a_ln_b"][l]),
        a_wm=jnp.where(causal[None], ws, jnp.zeros_like(ws)),
        a_ws=ws,
        a_bs=w["a_bs"][l],
        b_cw=w["b_conv_w"][l][:, None, :],
        b_cb=_lane_vec(w["b_conv_b"][l]),
        b_wax=jnp.concatenate([_block_diag(w["b_wa"][l]), _block_diag(w["b_wx"][l])], axis=1).astype(BF16),
        b_ba=_lane_vec(w["b_ba"][l]),
        b_bx=_lane_vec(w["b_bx"][l]),
        b_lam=_lane_vec(w["b_lam"][l]),
        c_ar=ar.reshape(S5_HALVES, S5_HALF_RI).reshape(1, S5_LANES // 2),
        c_ai=ai.reshape(S5_HALVES, S5_HALF_RI).reshape(1, S5_LANES // 2),
        c_bd=_s5_in_matrix(bbr, bbi),
        c_cd=_s5_out_matrix(w["c_c_re"][l], w["c_c_im"][l]),
        c_d=_lane_vec(w["c_d"][l]),
        c_glu_w=w["c_glu_w"][l].astype(BF16),
        c_glu_b=_lane_vec(w["c_glu_b"][l]),
        w_branch=w["w_branch"][l].astype(BF16),
        w_out=w["w_out"][l].astype(BF16),
        norm_ffn_g=_lane_vec(w["norm_ffn_g"][l]),
        final_norm_g=_lane_vec(w["final_norm_g"]),
    )
    j = l // 2
    if l % 2 == 0:
        lw.update(ffn_wg=w["ffn_w_gate"][j].astype(BF16), ffn_wu=w["ffn_w_up"][j].astype(BF16),
                  ffn_wd=w["ffn_w_down"][j].astype(BF16))
    else:
        pad = LANE_TILE - N_EXPERTS
        lw.update(moe_rw=jnp.pad(w["moe_router_w"][j], ((0, 0), (0, pad))),
                  moe_rb=jnp.pad(w["moe_router_b"][j], (0, pad)).reshape(1, LANE_TILE),
                  moe_wg=w["moe_w_gate"][j].astype(BF16), moe_wu=w["moe_w_up"][j].astype(BF16),
                  moe_wd=w["moe_w_down"][j].astype(BF16))
    return lw


def _trunk(x_tm, t, n, conv_s, lru_s, s5re_s, s5im_s, layers):
    x = x_tm
    vas, convs, lrus, sres, sims = [], [], [], [], []
    for l, lw in enumerate(layers):
        proj, gates = _in_proj(x, lw["norm_mix_g"], lw["w_in"], lw["b_gate"], lw["a_ln_g"], lw["a_ln_b"])
        if t % CHUNK == 0:
            bk = jnp.broadcast_to(lw["a_bs"].T[:, None, :, None], (CHUNK, n, H_A, HD_A))
            ya = _mixer_a_prompt(proj, t, n, lw["a_wm"].astype(BF16), bk.reshape(CHUNK * n, D_A))
        else:
            assert t < CHUNK
            wv = jnp.repeat(jnp.transpose(lw["a_ws"][:, :t, :t], (1, 2, 0)), HD_A, axis=-1)[:, :, None, :]
            bv = jnp.repeat(jnp.transpose(lw["a_bs"][:, :t], (1, 0)), HD_A, axis=-1)[:, None, :]
            ya = _mixer_a_sample(proj, t, n, wv, bv)
        yb, conv_new, h_last = _mixer_b(proj, t, n, jnp.swapaxes(conv_s[l], 0, 1), lru_s[l], lw["b_cw"],
                                        lw["b_cb"], lw["b_wax"], lw["b_ba"], lw["b_bx"], lw["b_lam"])
        yc, s_last = _mixer_c(proj, t, n, _s5_state_in(s5re_s[l], s5im_s[l]), lw["c_ar"], lw["c_ai"],
                              lw["c_bd"], lw["c_cd"], lw["c_d"], lw["c_glu_w"], lw["c_glu_b"])
        final = l == len(layers) - 1
        x1 = _merge(ya, yb, yc, gates, x, lw["w_branch"], lw["w_out"])
        if l % 2 == 0:
            x = _dense_ffn(x1, lw["norm_ffn_g"], lw["ffn_wg"], lw["ffn_wu"], lw["ffn_wd"],
                           lw["final_norm_g"], final)
        else:
            x = _moe_ffn(x1, lw["norm_ffn_g"], lw["moe_rw"], lw["moe_rb"], lw["moe_wg"], lw["moe_wu"],
                         lw["moe_wd"], lw["final_norm_g"], final)
        va = proj.reshape(t, n, PA_COLS)[:, :, BLK_VA * COL_BLK:(BLK_VA + 1) * COL_BLK]
        vas.append(jnp.swapaxes(va, 0, 1))
        convs.append(jnp.swapaxes(conv_new, 0, 1))
        lrus.append(h_last)
        s_re, s_im = _s5_state_out(s_last)
        sres.append(s_re)
        sims.append(s_im)
    return x, jnp.stack(vas), jnp.stack(convs), jnp.stack(lrus), jnp.stack(sres), jnp.stack(sims)


def kernel(x_prompt, x_sample, state_conv_b, state_lru_b, state_s5_re, state_s5_im, norm_mix_g, w_in, b_gate, a_ln_g, a_ln_b, a_ws, a_bs, b_conv_w, b_conv_b, b_wa, b_ba, b_wx, b_bx, b_lam, c_lam_re, c_lam_im, c_log_dt, c_b_re, c_b_im, c_c_re, c_c_im, c_d, c_glu_w, c_glu_b, w_branch, w_out, norm_ffn_g, ffn_w_gate, ffn_w_up, ffn_w_down, moe_router_w, moe_router_b, moe_w_gate, moe_w_up, moe_w_down, final_norm_g):
    w = dict(norm_mix_g=norm_mix_g, w_in=w_in, b_gate=b_gate, a_ln_g=a_ln_g, a_ln_b=a_ln_b, a_ws=a_ws,
             a_bs=a_bs, b_conv_w=b_conv_w, b_conv_b=b_conv_b, b_wa=b_wa, b_ba=b_ba, b_wx=b_wx, b_bx=b_bx,
             b_lam=b_lam, c_lam_re=c_lam_re, c_lam_im=c_lam_im, c_log_dt=c_log_dt, c_b_re=c_b_re,
             c_b_im=c_b_im, c_c_re=c_c_re, c_c_im=c_c_im, c_d=c_d, c_glu_w=c_glu_w, c_glu_b=c_glu_b,
             w_branch=w_branch, w_out=w_out, norm_ffn_g=norm_ffn_g, ffn_w_gate=ffn_w_gate,
             ffn_w_up=ffn_w_up, ffn_w_down=ffn_w_down, moe_router_w=moe_router_w,
             moe_router_b=moe_router_b, moe_w_gate=moe_w_gate, moe_w_up=moe_w_up, moe_w_down=moe_w_down,
             final_norm_g=final_norm_g)
    layers = [_layer_weights(l, w) for l in range(DEPTH)]

    nb, tp, _ = x_prompt.shape
    ns, ts, _ = x_sample.shape
    conv0 = jnp.zeros((DEPTH, nb, CONV_W - 1, D_B), F32)
    lru0 = jnp.zeros((DEPTH, nb, D_B), F32)
    s50 = jnp.zeros((DEPTH, nb, G_C, P_STATE), F32)
    xp = jnp.swapaxes(x_prompt, 0, 1).reshape(tp * nb, D_MODEL)
    yp, _, conv_p, lru_p, sre_p, sim_p = _trunk(xp, tp, nb, conv0, lru0, s50, s50, layers)
    xs = jnp.swapaxes(x_sample, 0, 1).reshape(ts * ns, D_MODEL)
    ys, va_s, conv_s, lru_s, sre_s, sim_s = _trunk(xs, ts, ns, state_conv_b, state_lru_b, state_s5_re,
                                                   state_s5_im, layers)
    y_prompt = jnp.swapaxes(yp.reshape(tp, nb, D_MODEL), 0, 1)
    y_sample = jnp.swapaxes(ys.reshape(ts, ns, D_MODEL), 0, 1)
    return (y_prompt, y_sample, conv_p, lru_p, sre_p, sim_p, va_s, conv_s, lru_s, sre_s, sim_s)
```

```python
import functools
import math

import jax
import jax.numpy as jnp
from jax import lax
from jax.experimental import pallas as pl
from jax.experimental.pallas import tpu as pltpu

D_MODEL = 1024
DEPTH = 2
CHUNK = 128
D_A = D_MODEL // 2
H_A = 4
HD_A = D_A // H_A
D_B = D_MODEL // 2
H_B = 8
BH_B = D_B // H_B
CONV_W = 4
LRU_C = 8.0
D_C = D_MODEL // 2
HG_C = 16
G_C = D_C // HG_C
P_STATE = 64
N_BRANCH = 3
D_BR = D_MODEL // 2
IN_COLS = 2 * D_A + 2 * D_B + D_C + N_BRANCH * D_MODEL
D_FF = 11 * D_MODEL // 4
N_EXPERTS = 8
TOP_K = 2
D_EXP = D_FF // 2
EPS = 1e-6

F32 = jnp.float32
BF16 = jnp.bfloat16

COL_BLK = 512
N_COL_BLK = IN_COLS // COL_BLK
BLK_UA, BLK_VA, BLK_XB, BLK_GB, BLK_UC, BLK_GATE = 0, 1, 2, 3, 4, 5
PA_COLS = BLK_GATE * COL_BLK
S5_HALVES = 2
S5_HALF_IN = D_C // S5_HALVES
S5_HALF_RI = (G_C // S5_HALVES) * P_STATE
S5_HALF_ST = 2 * S5_HALF_RI
S5_LANES = S5_HALVES * S5_HALF_ST
LANE_TILE = 128
SUBLANE_TILE = 8
MXU_DIM = 256
FFN_CHUNK = 3 * MXU_DIM
VMEM_LIMIT = 56 * 1024 * 1024


def _resident(shape):
    zeros = (0,) * len(shape)
    return pl.BlockSpec(shape, lambda *_: zeros)


def _params(sem):
    return pltpu.CompilerParams(dimension_semantics=sem, vmem_limit_bytes=VMEM_LIMIT)


def _rms(x, g):
    return x * lax.rsqrt(jnp.mean(x * x, axis=-1, keepdims=True) + EPS) * g


def _bdot(a, b):
    return jnp.dot(a.astype(BF16), b, preferred_element_type=F32)


def _in_proj_kernel(x_ref, g_ref, w_ref, bg_ref, lng_ref, lnb_ref, pa_ref, gate_ref):
    hb = _rms(x_ref[...], g_ref[...]).astype(BF16)
    for j in range(N_COL_BLK):
        cols = slice(j * COL_BLK, (j + 1) * COL_BLK)
        acc = jnp.dot(hb, w_ref[:, cols], preferred_element_type=F32)
        if j in (BLK_UA, BLK_GB):
            pa_ref[:, cols] = jax.nn.gelu(acc)
        elif j == BLK_VA:
            v = jax.nn.gelu(acc)
            vc = v - jnp.mean(v, axis=-1, keepdims=True)
            var = jnp.mean(vc * vc, axis=-1, keepdims=True)
            pa_ref[:, cols] = vc * lax.rsqrt(var + EPS) * lng_ref[...] + lnb_ref[...]
        elif j in (BLK_XB, BLK_UC):
            pa_ref[:, cols] = acc
        else:
            gcols = slice(cols.start - PA_COLS, cols.stop - PA_COLS)
            gate_ref[:, gcols] = jax.nn.sigmoid(acc + bg_ref[:, gcols]).astype(BF16)


def _in_proj(x, norm_g, w_in, b_gate, ln_g, ln_b):
    r = x.shape[0]
    tm = min(r, 512)
    return pl.pallas_call(
        _in_proj_kernel,
        grid=(r // tm,),
        in_specs=[
            pl.BlockSpec((tm, D_MODEL), lambda i: (i, 0)),
            _resident((1, D_MODEL)),
            _single((D_MODEL, IN_COLS)),
            _resident((1, N_BRANCH * D_MODEL)),
            _resident((1, D_A)),
            _resident((1, D_A)),
        ],
        out_specs=[
            pl.BlockSpec((tm, PA_COLS), lambda i: (i, 0)),
            pl.BlockSpec((tm, N_BRANCH * D_MODEL), lambda i: (i, 0)),
        ],
        out_shape=[
            jax.ShapeDtypeStruct((r, PA_COLS), F32),
            jax.ShapeDtypeStruct((r, N_BRANCH * D_MODEL), BF16),
        ],
        compiler_params=_params(("parallel",)),
        name="in_proj",
    )(x, norm_g, w_in, b_gate, ln_g, ln_b)


def _mixer_a_mxu_kernel(u_ref, v_ref, wm_ref, bk_ref, o_ref, wk_scr, *, rows, n):
    @pl.when(pl.program_id(0) == 0)
    def _():
        shift = n.bit_length() - 1
        e_rows = (lax.broadcasted_iota(jnp.int32, (rows, CHUNK), 0) >> shift
                  == lax.broadcasted_iota(jnp.int32, (rows, CHUNK), 1)).astype(BF16)
        e_cols = (lax.broadcasted_iota(jnp.int32, (CHUNK, rows), 0)
                  == lax.broadcasted_iota(jnp.int32, (CHUNK, rows), 1) >> shift).astype(BF16)
        same_seq = ((lax.broadcasted_iota(jnp.int32, (rows, rows), 0) & (n - 1))
                    == (lax.broadcasted_iota(jnp.int32, (rows, rows), 1) & (n - 1)))
        for h in range(H_A):
            left = jnp.dot(e_rows, wm_ref[h], preferred_element_type=F32).astype(BF16)
            full = jnp.dot(left, e_cols, preferred_element_type=F32)
            wk_scr[h] = jnp.where(same_seq, full, 0.0).astype(BF16)

    vb = v_ref[...].astype(BF16)
    for h in range(H_A):
        cols = slice(h * HD_A, (h + 1) * HD_A)
        for r0 in range(0, rows, MXU_DIM):
            k = r0 + MXU_DIM
            mixed = jnp.dot(wk_scr[h, r0:k, 0:k], vb[0:k, cols], preferred_element_type=F32)
            o_ref[r0:k, cols] = (u_ref[r0:k, cols] * (mixed + bk_ref[r0:k, cols])).astype(BF16)


def _mixer_a_prompt(pa, t, n, wm, bk):
    rows = CHUNK * n
    assert rows % MXU_DIM == 0 and MXU_DIM % n == 0 and n & (n - 1) == 0
    return pl.pallas_call(
        functools.partial(_mixer_a_mxu_kernel, rows=rows, n=n),
        grid=(t // CHUNK,),
        in_specs=[
            pl.BlockSpec((rows, COL_BLK), lambda c: (c, BLK_UA)),
            pl.BlockSpec((rows, COL_BLK), lambda c: (c, BLK_VA)),
            _single((H_A, CHUNK, CHUNK)),
            _single((rows, D_A)),
        ],
        out_specs=pl.BlockSpec((rows, D_A), lambda c: (c, 0)),
        out_shape=jax.ShapeDtypeStruct((t * n, D_A), BF16),
        scratch_shapes=[pltpu.VMEM((H_A, rows, rows), BF16)],
        compiler_params=_params(("arbitrary",)),
        name="mixer_a_prompt",
    )(pa, pa, wm, bk)


def _mixer_a_vpu_kernel(u_ref, v_ref, wv_ref, bv_ref, o_ref, *, t, n):
    for i in range(t):
        mixed = wv_ref[i, 0] * v_ref[0]
        for s in range(1, i + 1):
            mixed = mixed + wv_ref[i, s] * v_ref[s]
        o_ref[i * n:(i + 1) * n] = (u_ref[i] * (mixed + bv_ref[i])).astype(BF16)


def _mixer_a_sample(proj, t, n, wv, bv):
    proj3 = proj.reshape(t, n, PA_COLS)
    return pl.pallas_call(
        functools.partial(_mixer_a_vpu_kernel, t=t, n=n),
        grid=(1,),
        in_specs=[
            pl.BlockSpec((t, n, COL_BLK), lambda i: (0, 0, BLK_UA)),
            pl.BlockSpec((t, n, COL_BLK), lambda i: (0, 0, BLK_VA)),
            _resident((t, t, 1, D_A)),
            _resident((t, 1, D_A)),
        ],
        out_specs=_resident((t * n, D_A)),
        out_shape=jax.ShapeDtypeStruct((t * n, D_A), BF16),
        compiler_params=_params(("arbitrary",)),
        name="mixer_a_sample",
    )(proj3, proj3, wv, bv)


def _lru_kernel(xb_ref, gb_ref, conv0_ref, h0_ref, cw_ref, cb_ref, wax_ref, ba_ref, bx_ref, lam_ref,
                yb_ref, convn_ref, hlast_ref, xp_scr, a_scr, d_scr, h_scr, *, tt, n):
    @pl.when(pl.program_id(0) == 0)
    def _():
        xp_scr[0:CONV_W - 1] = conv0_ref[...]
        h_scr[...] = h0_ref[...]

    xp_scr[CONV_W - 1:CONV_W - 1 + tt] = xb_ref[...]
    xc = cb_ref[...] + xp_scr[0:tt] * cw_ref[0]
    for k in range(1, CONV_W):
        xc = xc + xp_scr[k:k + tt] * cw_ref[k]
    tail = xp_scr[tt:tt + CONV_W - 1]
    convn_ref[...] = tail
    xp_scr[0:CONV_W - 1] = tail

    xc2 = xc.reshape(tt * n, D_B)
    pre = _bdot(xc2, wax_ref[...])
    r = jax.nn.sigmoid(pre[:, :D_B] + ba_ref[...])
    i = jax.nn.sigmoid(pre[:, D_B:] + bx_ref[...])
    log_a = -LRU_C * r * jax.nn.softplus(-lam_ref[...])
    a_scr[...] = jnp.exp(log_a).reshape(tt, n, D_B)
    gain = jnp.sqrt(-jnp.tanh(log_a) * (jnp.exp(2.0 * log_a) + 1.0))
    d_scr[...] = (gain * (i * xc2)).reshape(tt, n, D_B)

    def step(s, h):
        h = a_scr[s] * h + d_scr[s]
        d_scr[s] = h
        return h

    h = lax.fori_loop(0, tt, step, h_scr[...], unroll=min(tt, 8))
    h_scr[...] = h
    hlast_ref[...] = h
    yb_ref[...] = (d_scr[...] * gb_ref[...]).reshape(tt * n, D_B).astype(BF16)


def _mixer_b(proj, t, n, conv0, h0, cw, cb, wax, ba, bx, lam):
    tt = min(t, 512 // n) if n <= 512 else 1
    proj3 = proj.reshape(t, n, PA_COLS)
    yb, conv_new, h_last = pl.pallas_call(
        functools.partial(_lru_kernel, tt=tt, n=n),
        grid=(t // tt,),
        in_specs=[
            pl.BlockSpec((tt, n, COL_BLK), lambda i: (i, 0, BLK_XB)),
            pl.BlockSpec((tt, n, COL_BLK), lambda i: (i, 0, BLK_GB)),
            _resident((CONV_W - 1, n, D_B)),
            _resident((n, D_B)),
            _resident((CONV_W, 1, D_B)),
            _resident((1, D_B)),
            _resident((D_B, 2 * D_B)),
            _resident((1, D_B)),
            _resident((1, D_B)),
            _resident((1, D_B)),
        ],
        out_specs=[
            pl.BlockSpec((tt * n, D_B), lambda i: (i, 0)),
            _resident((CONV_W - 1, n, D_B)),
            _resident((n, D_B)),
        ],
        out_shape=[
            jax.ShapeDtypeStruct((t * n, D_B), BF16),
            jax.ShapeDtypeStruct((CONV_W - 1, n, D_B), F32),
            jax.ShapeDtypeStruct((n, D_B), F32),
        ],
        scratch_shapes=[
            pltpu.VMEM((tt + CONV_W - 1, n, D_B), F32),
            pltpu.VMEM((tt, n, D_B), F32),
            pltpu.VMEM((tt, n, D_B), F32),
            pltpu.VMEM((n, D_B), F32),
        ],
        compiler_params=_params(("arbitrary",)),
        name="mixer_b",
    )(proj3, proj3, conv0, h0, cw, cb, wax, ba, bx, lam)
    return yb, conv_new, h_last


def _s5_disc_kernel(lr_ref, li_ref, ldt_ref, bre_ref, bim_ref, ar_ref, ai_ref, bbr_ref, bbi_ref):
    lr = lr_ref[...]
    li = li_ref[...]
    dt = jnp.exp(ldt_ref[...])
    mag = jnp.exp(lr * dt)
    ar = mag * jnp.cos(li * dt)
    ai = mag * jnp.sin(li * dt)
    ar_ref[...] = ar
    ai_ref[...] = ai
    den = lr * lr + li * li
    qr = ((ar - 1.0) * lr + ai * li) / den
    qi = (ai * lr - (ar - 1.0) * li) / den
    bre = bre_ref[...]
    bim = bim_ref[...]
    bbr_ref[...] = qr * bre - qi * bim
    bbi_ref[...] = qr * bim + qi * bre


def _s5_discretize(lam_re, lam_im, log_dt, b_re, b_im):
    g3 = jax.ShapeDtypeStruct((G_C, 1, P_STATE), F32)
    b3 = jax.ShapeDtypeStruct((G_C, HG_C, P_STATE), F32)
    return pl.pallas_call(
        _s5_disc_kernel,
        out_shape=[g3, g3, b3, b3],
        name="s5_discretize",
    )(lam_re.reshape(G_C, 1, P_STATE), lam_im.reshape(G_C, 1, P_STATE), log_dt.reshape(G_C, 1, 1),
      jnp.swapaxes(b_re, 1, 2), jnp.swapaxes(b_im, 1, 2))


def _s5_kernel(u_ref, s0_ref, ar_ref, ai_ref, bd_ref, cd_ref, d_ref, gw_ref, gb_ref,
               yc_ref, slast_ref, bu_scr, st_scr, *, tt, n):
    @pl.when(pl.program_id(0) == 0)
    def _():
        st_scr[...] = s0_ref[...]

    u2 = u_ref[...].reshape(tt * n, D_C)
    ub = u2.astype(BF16)
    for hf in range(S5_HALVES):
        bu = jnp.dot(ub[:, hf * S5_HALF_IN:(hf + 1) * S5_HALF_IN], bd_ref[hf], preferred_element_type=F32)
        bu_scr[:, :, hf * S5_HALF_ST:(hf + 1) * S5_HALF_ST] = bu.reshape(tt, n, S5_HALF_ST)

    lane_blk = 4 * LANE_TILE
    for hf in range(S5_HALVES):
        for q in range(S5_HALF_RI // lane_blk):
            lr = hf * S5_HALF_ST + q * lane_blk
            li = lr + S5_HALF_RI
            la = hf * S5_HALF_RI + q * lane_blk
            ar = jnp.broadcast_to(ar_ref[:, la:la + lane_blk], (SUBLANE_TILE, lane_blk))
            ai = jnp.broadcast_to(ai_ref[:, la:la + lane_blk], (SUBLANE_TILE, lane_blk))

            def rows_body(rg, carry, lr=lr, li=li, ar=ar, ai=ai):
                rs = pl.ds(pl.multiple_of(rg * SUBLANE_TILE, SUBLANE_TILE), SUBLANE_TILE)

                def step(s, x):
                    xr, xi = x
                    nr = ar * xr - ai * xi + bu_scr[s, rs, lr:lr + lane_blk]
                    ni = ar * xi + ai * xr + bu_scr[s, rs, li:li + lane_blk]
                    bu_scr[s, rs, lr:lr + lane_blk] = nr
                    bu_scr[s, rs, li:li + lane_blk] = ni
                    return nr, ni

                x0 = (st_scr[rs, lr:lr + lane_blk], st_scr[rs, li:li + lane_blk])
                xr, xi = lax.fori_loop(0, tt, step, x0, unroll=min(tt, 8))
                st_scr[rs, lr:lr + lane_blk] = xr
                st_scr[rs, li:li + lane_blk] = xi
                return carry

            lax.fori_loop(0, n // SUBLANE_TILE, rows_body, 0)

    slast_ref[...] = st_scr[...]
    ys = []
    for hf in range(S5_HALVES):
        xs = bu_scr[:, :, hf * S5_HALF_ST:(hf + 1) * S5_HALF_ST].reshape(tt * n, S5_HALF_ST)
        ys.append(_bdot(xs, cd_ref[hf]))
    y = jnp.concatenate(ys, axis=-1) + d_ref[...] * u2
    z = jax.nn.gelu(y)
    yc = z * jax.nn.sigmoid(_bdot(z, gw_ref[...]) + gb_ref[...])
    yc_ref[...] = yc.astype(BF16)


def _mixer_c(proj, t, n, s0, ar, ai, bd, cd, d, glu_w, glu_b):
    tt = min(t, 512 // n) if n <= 512 else 1
    proj3 = proj.reshape(t, n, PA_COLS)
    yc, s_last = pl.pallas_call(
        functools.partial(_s5_kernel, tt=tt, n=n),
        grid=(t // tt,),
        in_specs=[
            pl.BlockSpec((tt, n, COL_BLK), lambda i: (i, 0, BLK_UC)),
            _resident((n, S5_LANES)),
            _resident((1, S5_LANES // 2)),
            _resident((1, S5_LANES // 2)),
            _resident((S5_HALVES, S5_HALF_IN, S5_HALF_ST)),
            _resident((S5_HALVES, S5_HALF_ST, S5_HALF_IN)),
            _resident((1, D_C)),
            _resident((D_C, D_C)),
            _resident((1, D_C)),
        ],
        out_specs=[
            pl.BlockSpec((tt * n, D_C), lambda i: (i, 0)),
            _resident((n, S5_LANES)),
        ],
        out_shape=[
            jax.ShapeDtypeStruct((t * n, D_C), BF16),
            jax.ShapeDtypeStruct((n, S5_LANES), F32),
        ],
        scratch_shapes=[
            pltpu.VMEM((tt, n, S5_LANES), F32),
            pltpu.VMEM((n, S5_LANES), F32),
        ],
        compiler_params=_params(("arbitrary",)),
        name="mixer_c",
    )(proj3, s0, ar, ai, bd, cd, d, glu_w, glu_b)
    return yc, s_last


def _single(shape):
    zeros = (0,) * len(shape)
    return pl.BlockSpec(shape, lambda *_: zeros, pipeline_mode=pl.Buffered(1))


def _merge_kernel(ya_ref, yb_ref, yc_ref, g_ref, x_ref, wbr_ref, wout_ref, o_ref):
    m = None
    for k, y_ref in enumerate((ya_ref, yb_ref, yc_ref)):
        p = _bdot(y_ref[...], wbr_ref[k])
        g = g_ref[:, k * D_MODEL:(k + 1) * D_MODEL].astype(F32)
        m = g * p if m is None else m + g * p
    o_ref[...] = x_ref[...] + _bdot(m, wout_ref[...])


def _merge(ya, yb, yc, gates, x, wbr, wout):
    r = x.shape[0]
    tm = min(r, 512)
    row = pl.BlockSpec((tm, D_MODEL), lambda i: (i, 0))
    return pl.pallas_call(
        _merge_kernel,
        grid=(r // tm,),
        in_specs=[pl.BlockSpec((tm, D_BR), lambda i: (i, 0))] * N_BRANCH + [
            pl.BlockSpec((tm, N_BRANCH * D_MODEL), lambda i: (i, 0)),
            row,
            _single((N_BRANCH, D_BR, D_MODEL)),
            _single((D_MODEL, D_MODEL)),
        ],
        out_specs=row,
        out_shape=jax.ShapeDtypeStruct((r, D_MODEL), F32),
        compiler_params=_params(("parallel",)),
        name="merge",
    )(ya, yb, yc, gates, x, wbr, wout)


def _swiglu(hb, wg_ref, wu_ref, wd_ref):
    hidden = wg_ref.shape[-1]
    out = None
    for c0 in range(0, hidden, FFN_CHUNK):
        cols = slice(c0, min(c0 + FFN_CHUNK, hidden))
        gate = jnp.dot(hb, wg_ref[:, cols], preferred_element_type=F32)
        up = jnp.dot(hb, wu_ref[:, cols], preferred_element_type=F32)
        part = _bdot(jax.nn.silu(gate) * up, wd_ref[cols, :])
        out = part if out is None else out + part
    return out


def _dense_ffn_kernel(x_ref, nf_ref, wg_ref, wu_ref, wd_ref, fin_ref, o_ref, *, final):
    x1 = x_ref[...]
    hb = _rms(x1, nf_ref[...]).astype(BF16)
    x2 = x1 + _swiglu(hb, wg_ref, wu_ref, wd_ref)
    o_ref[...] = _rms(x2, fin_ref[...]) if final else x2


def _dense_ffn(x1, nf, wg, wu, wd, fin, final):
    r = x1.shape[0]
    tm = min(r, 1024)
    row = pl.BlockSpec((tm, D_MODEL), lambda i: (i, 0))
    return pl.pallas_call(
        functools.partial(_dense_ffn_kernel, final=final),
        grid=(r // tm,),
        in_specs=[
            row,
            _resident((1, D_MODEL)),
            _single((D_MODEL, D_FF)),
            _single((D_MODEL, D_FF)),
            _single((D_FF, D_MODEL)),
            _resident((1, D_MODEL)),
        ],
        out_specs=row,
        out_shape=jax.ShapeDtypeStruct((r, D_MODEL), F32),
        compiler_params=_params(("parallel",)),
        name="dense_ffn",
    )(x1, nf, wg, wu, wd, fin)


def _moe_kernel(x_ref, nf_ref, rw_ref, rb_ref, wg_ref, wu_ref, wd_ref, fin_ref, o_ref,
                hb_scr, comb_scr, acc_scr, *, final):
    e = pl.program_id(1)
    lane = lax.broadcasted_iota(jnp.int32, comb_scr.shape, 1)

    @pl.when(e == 0)
    def _():
        hn = _rms(x_ref[...], nf_ref[...])
        hb_scr[...] = hn.astype(BF16)
        logits = jnp.dot(hn, rw_ref[...], preferred_element_type=F32,
                         precision=lax.Precision.HIGHEST) + rb_ref[...]
        neg = jnp.float32(-jnp.inf)
        lg = jnp.where(lane < N_EXPERTS, logits, neg)
        m1 = jnp.max(lg, axis=-1, keepdims=True)
        i1 = jnp.min(jnp.where(lg == m1, lane, LANE_TILE), axis=-1, keepdims=True)
        lg2 = jnp.where(lane == i1, neg, lg)
        m2 = jnp.max(lg2, axis=-1, keepdims=True)
        i2 = jnp.min(jnp.where(lg2 == m2, lane, LANE_TILE), axis=-1, keepdims=True)
        e2 = jnp.exp(m2 - m1)
        den = 1.0 + e2
        comb_scr[...] = jnp.where(lane == i1, 1.0 / den, 0.0) + jnp.where(lane == i2, e2 / den, 0.0)
        acc_scr[...] = jnp.zeros_like(acc_scr)

    out_e = _swiglu(hb_scr[...], wg_ref, wu_ref, wd_ref)
    w_e = jnp.sum(jnp.where(lane == e, comb_scr[...], 0.0), axis=-1, keepdims=True)
    acc_scr[...] += w_e * out_e

    @pl.when(e == N_EXPERTS - 1)
    def _():
        x2 = x_ref[...] + acc_scr[...]
        o_ref[...] = _rms(x2, fin_ref[...]) if final else x2


def _moe_ffn(x1, nf, rw, rb, wg, wu, wd, fin, final):
    r = x1.shape[0]
    tm = min(r, 1024)
    row = pl.BlockSpec((tm, D_MODEL), lambda i, e: (i, 0))
    return pl.pallas_call(
        functools.partial(_moe_kernel, final=final),
        grid=(r // tm, N_EXPERTS),
        in_specs=[
            row,
            _resident((1, D_MODEL)),
            _single((D_MODEL, LANE_TILE)),
            _resident((1, LANE_TILE)),
            pl.BlockSpec((None, D_MODEL, D_EXP), lambda i, e: (e, 0, 0)),
            pl.BlockSpec((None, D_MODEL, D_EXP), lambda i, e: (e, 0, 0)),
            pl.BlockSpec((None, D_EXP, D_MODEL), lambda i, e: (e, 0, 0)),
            _resident((1, D_MODEL)),
        ],
        out_specs=row,
        out_shape=jax.ShapeDtypeStruct((r, D_MODEL), F32),
        scratch_shapes=[
            pltpu.VMEM((tm, D_MODEL), BF16),
            pltpu.VMEM((tm, LANE_TILE), F32),
            pltpu.VMEM((tm, D_MODEL), F32),
        ],
        compiler_params=_params(("parallel", "arbitrary")),
        name="moe_ffn",
    )(x1, nf, rw, rb, wg, wu, wd, fin)


def _block_diag(w):
    h, i, j = w.shape
    eye = jnp.eye(h, dtype=w.dtype)
    return jnp.einsum("hij,hk->hikj", w, eye).reshape(h * i, h * j)


def _s5_in_matrix(bbr, bbi):
    gh = G_C // S5_HALVES
    out = []
    for part in (bbr, bbi):
        p4 = part.reshape(S5_HALVES, gh, HG_C, P_STATE)
        eye = jnp.eye(gh, dtype=part.dtype)
        out.append(jnp.einsum("aghp,gk->aghkp", p4, eye).reshape(S5_HALVES, gh * HG_C, gh * P_STATE))
    return jnp.concatenate(out, axis=-1).astype(BF16)


def _s5_out_matrix(c_re, c_im):
    gh = G_C // S5_HALVES
    out = []
    for part in (c_re, -c_im):
        p4 = part.reshape(S5_HALVES, gh, HG_C, P_STATE)
        eye = jnp.eye(gh, dtype=part.dtype)
        out.append(jnp.einsum("agop,gk->agpko", p4, eye).reshape(S5_HALVES, gh * P_STATE, gh * HG_C))
    return jnp.concatenate(out, axis=1).astype(BF16)


def _s5_state_in(re, im):
    n = re.shape[0]
    parts = [a.reshape(n, S5_HALVES, 1, S5_HALF_RI) for a in (re, im)]
    return jnp.concatenate(parts, axis=2).reshape(n, S5_LANES)


def _s5_state_out(s):
    n = s.shape[0]
    s4 = s.reshape(n, S5_HALVES, 2, S5_HALF_RI)
    return s4[:, :, 0].reshape(n, G_C, P_STATE), s4[:, :, 1].reshape(n, G_C, P_STATE)


def _lane_vec(v):
    return v.reshape(1, -1)


def _layer_weights(l, w):
    ar, ai, bbr, bbi = _s5_discretize(w["c_lam_re"][l], w["c_lam_im"][l], w["c_log_dt"][l],
                                      w["c_b_re"][l], w["c_b_im"][l])
    causal = jnp.tril(jnp.ones((CHUNK, CHUNK), dtype=bool))
    ws = w["a_ws"][l]
    lw = dict(
        norm_mix_g=_lane_vec(w["norm_mix_g"][l]),
        w_in=w["w_in"][l].astype(BF16),
        b_gate=_lane_vec(w["b_gate"][l]),
        a_ln_g=_lane_vec(w["a_ln_g"][l]),
        a_ln_b=_lane_vec(w["a_ln_b"][l]),
        a_wm=jnp.where(causal[None], ws, jnp.zeros_like(ws)),
        a_ws=ws,
        a_bs=w["a_bs"][l],
        b_cw=w["b_conv_w"][l][:, None, :],
        b_cb=_lane_vec(w["b_conv_b"][l]),
        b_wax=jnp.concatenate([_block_diag(w["b_wa"][l]), _block_diag(w["b_wx"][l])], axis=1).astype(BF16),
        b_ba=_lane_vec(w["b_ba"][l]),
        b_bx=_lane_vec(w["b_bx"][l]),
        b_lam=_lane_vec(w["b_lam"][l]),
        c_ar=ar.reshape(S5_HALVES, S5_HALF_RI).reshape(1, S5_LANES // 2),
        c_ai=ai.reshape(S5_HALVES, S5_HALF_RI).reshape(1, S5_LANES // 2),
        c_bd=_s5_in_matrix(bbr, bbi),
        c_cd=_s5_out_matrix(w["c_c_re"][l], w["c_c_im"][l]),
        c_d=_lane_vec(w["c_d"][l]),
        c_glu_w=w["c_glu_w"][l].astype(BF16),
        c_glu_b=_lane_vec(w["c_glu_b"][l]),
        w_branch=w["w_branch"][l].astype(BF16),
        w_out=w["w_out"][l].astype(BF16),
        norm_ffn_g=_lane_vec(w["norm_ffn_g"][l]),
        final_norm_g=_lane_vec(w["final_norm_g"]),
    )
    j = l // 2
    if l % 2 == 0:
        lw.update(ffn_wg=w["ffn_w_gate"][j].astype(BF16), ffn_wu=w["ffn_w_up"][j].astype(BF16),
                  ffn_wd=w["ffn_w_down"][j].astype(BF16))
    else:
        pad = LANE_TILE - N_EXPERTS
        lw.update(moe_rw=jnp.pad(w["moe_router_w"][j], ((0, 0), (0, pad))),
                  moe_rb=jnp.pad(w["moe_router_b"][j], (0, pad)).reshape(1, LANE_TILE),
                  moe_wg=w["moe_w_gate"][j].astype(BF16), moe_wu=w["moe_w_up"][j].astype(BF16),
                  moe_wd=w["moe_w_down"][j].astype(BF16))
    return lw


def _trunk(x_tm, t, n, conv_s, lru_s, s5re_s, s5im_s, layers):
    x = x_tm
    vas, convs, lrus, sres, sims = [], [], [], [], []
    for l, lw in enumerate(layers):
        proj, gates = _in_proj(x, lw["norm_mix_g"], lw["w_in"], lw["b_gate"], lw["a_ln_g"], lw["a_ln_b"])
        if t % CHUNK == 0:
            bk = jnp.broadcast_to(lw["a_bs"].T[:, None, :, None], (CHUNK, n, H_A, HD_A))
            ya = _mixer_a_prompt(proj, t, n, lw["a_wm"].astype(BF16), bk.reshape(CHUNK * n, D_A))
        else:
            assert t < CHUNK
            wv = jnp.repeat(jnp.transpose(lw["a_ws"][:, :t, :t], (1, 2, 0)), HD_A, axis=-1)[:, :, None, :]
            bv = jnp.repeat(jnp.transpose(lw["a_bs"][:, :t], (1, 0)), HD_A, axis=-1)[:, None, :]
            ya = _mixer_a_sample(proj, t, n, wv, bv)
        yb, conv_new, h_last = _mixer_b(proj, t, n, jnp.swapaxes(conv_s[l], 0, 1), lru_s[l], lw["b_cw"],
                                        lw["b_cb"], lw["b_wax"], lw["b_ba"], lw["b_bx"], lw["b_lam"])
        yc, s_last = _mixer_c(proj, t, n, _s5_state_in(s5re_s[l], s5im_s[l]), lw["c_ar"], lw["c_ai"],
                              lw["c_bd"], lw["c_cd"], lw["c_d"], lw["c_glu_w"], lw["c_glu_b"])
        final = l == len(layers) - 1
        x1 = _merge(ya, yb, yc, gates, x, lw["w_branch"], lw["w_out"])
        if l % 2 == 0:
            x = _dense_ffn(x1, lw["norm_ffn_g"], lw["ffn_wg"], lw["ffn_wu"], lw["ffn_wd"],
                           lw["final_norm_g"], final)
        else:
            x = _moe_ffn(x1, lw["norm_ffn_g"], lw["moe_rw"], lw["moe_rb"], lw["moe_wg"], lw["moe_wu"],
                         lw["moe_wd"], lw["final_norm_g"], final)
        va = proj.reshape(t, n, PA_COLS)[:, :, BLK_VA * COL_BLK:(BLK_VA + 1) * COL_BLK]
        vas.append(jnp.swapaxes(va, 0, 1))
        convs.append(jnp.swapaxes(conv_new, 0, 1))
        lrus.append(h_last)
        s_re, s_im = _s5_state_out(s_last)
        sres.append(s_re)
        sims.append(s_im)
    return x, jnp.stack(vas), jnp.stack(convs), jnp.stack(lrus), jnp.stack(sres), jnp.stack(sims)


def kernel(x_prompt, x_sample, state_conv_b, state_lru_b, state_s5_re, state_s5_im, norm_mix_g, w_in, b_gate, a_ln_g, a_ln_b, a_ws, a_bs, b_conv_w, b_conv_b, b_wa, b_ba, b_wx, b_bx, b_lam, c_lam_re, c_lam_im, c_log_dt, c_b_re, c_b_im, c_c_re, c_c_im, c_d, c_glu_w, c_glu_b, w_branch, w_out, norm_ffn_g, ffn_w_gate, ffn_w_up, ffn_w_down, moe_router_w, moe_router_b, moe_w_gate, moe_w_up, moe_w_down, final_norm_g):
    w = dict(norm_mix_g=norm_mix_g, w_in=w_in, b_gate=b_gate, a_ln_g=a_ln_g, a_ln_b=a_ln_b, a_ws=a_ws,
             a_bs=a_bs, b_conv_w=b_conv_w, b_conv_b=b_conv_b, b_wa=b_wa, b_ba=b_ba, b_wx=b_wx, b_bx=b_bx,
             b_lam=b_lam, c_lam_re=c_lam_re, c_lam_im=c_lam_im, c_log_dt=c_log_dt, c_b_re=c_b_re,
             c_b_im=c_b_im, c_c_re=c_c_re, c_c_im=c_c_im, c_d=c_d, c_glu_w=c_glu_w, c_glu_b=c_glu_b,
             w_branch=w_branch, w_out=w_out, norm_ffn_g=norm_ffn_g, ffn_w_gate=ffn_w_gate,
             ffn_w_up=ffn_w_up, ffn_w_down=ffn_w_down, moe_router_w=moe_router_w,
             moe_router_b=moe_router_b, moe_w_gate=moe_w_gate, moe_w_up=moe_w_up, moe_w_down=moe_w_down,
             final_norm_g=final_norm_g)
    layers = [_layer_weights(l, w) for l in range(DEPTH)]

    nb, tp, _ = x_prompt.shape
    ns, ts, _ = x_sample.shape
    conv0 = jnp.zeros((DEPTH, nb, CONV_W - 1, D_B), F32)
    lru0 = jnp.zeros((DEPTH, nb, D_B), F32)
    s50 = jnp.zeros((DEPTH, nb, G_C, P_STATE), F32)
    xp = jnp.swapaxes(x_prompt, 0, 1).reshape(tp * nb, D_MODEL)
    yp, _, conv_p, lru_p, sre_p, sim_p = _trunk(xp, tp, nb, conv0, lru0, s50, s50, layers)
    xs = jnp.swapaxes(x_sample, 0, 1).reshape(ts * ns, D_MODEL)
    ys, va_s, conv_s, lru_s, sre_s, sim_s = _trunk(xs, ts, ns, state_conv_b, state_lru_b, state_s5_re,
                                                   state_s5_im, layers)
    y_prompt = jnp.swapaxes(yp.reshape(tp, nb, D_MODEL), 0, 1)
    y_sample = jnp.swapaxes(ys.reshape(ts, ns, D_MODEL), 0, 1)
    return (y_prompt, y_sample, conv_p, lru_p, sre_p, sim_p, va_s, conv_s, lru_s, sre_s, sim_s)
```

```python
import functools
import math

import jax
import jax.numpy as jnp
from jax import lax
from jax.experimental import pallas as pl
from jax.experimental.pallas import tpu as pltpu

D_MODEL = 1024
DEPTH = 2
CHUNK = 128
D_A = D_MODEL // 2
H_A = 4
HD_A = D_A // H_A
D_B = D_MODEL // 2
H_B = 8
BH_B = D_B // H_B
CONV_W = 4
LRU_C = 8.0
D_C = D_MODEL // 2
HG_C = 16
G_C = D_C // HG_C
P_STATE = 64
N_BRANCH = 3
D_BR = D_MODEL // 2
IN_COLS = 2 * D_A + 2 * D_B + D_C + N_BRANCH * D_MODEL
D_FF = 11 * D_MODEL // 4
N_EXPERTS = 8
TOP_K = 2
D_EXP = D_FF // 2
EPS = 1e-6

F32 = jnp.float32
BF16 = jnp.bfloat16

COL_BLK = 512
N_COL_BLK = IN_COLS // COL_BLK
BLK_UA, BLK_VA, BLK_XB, BLK_GB, BLK_UC, BLK_GATE = 0, 1, 2, 3, 4, 5
PA_COLS = BLK_GATE * COL_BLK
S5_HALVES = 2
S5_HALF_IN = D_C // S5_HALVES
S5_HALF_RI = (G_C // S5_HALVES) * P_STATE
S5_HALF_ST = 2 * S5_HALF_RI
S5_LANES = S5_HALVES * S5_HALF_ST
LANE_TILE = 128
SUBLANE_TILE = 8
MXU_DIM = 256
FFN_CHUNK = 3 * MXU_DIM
VMEM_LIMIT = 56 * 1024 * 1024


def _resident(shape):
    zeros = (0,) * len(shape)
    return pl.BlockSpec(shape, lambda *_: zeros)


def _params(sem):
    return pltpu.CompilerParams(dimension_semantics=sem, vmem_limit_bytes=VMEM_LIMIT)


def _rms(x, g):
    return x * lax.rsqrt(jnp.mean(x * x, axis=-1, keepdims=True) + EPS) * g


def _bdot(a, b):
    return jnp.dot(a.astype(BF16), b, preferred_element_type=F32)


def _in_proj_kernel(x_ref, g_ref, w_ref, bg_ref, lng_ref, lnb_ref, pa_ref, gate_ref):
    half = x_ref.shape[0] // 2
    for r0 in (0, half):
        rows = slice(r0, r0 + half)
        hb = _rms(x_ref[rows, :], g_ref[...]).astype(BF16)
        for j in range(N_COL_BLK):
            cols = slice(j * COL_BLK, (j + 1) * COL_BLK)
            acc = jnp.dot(hb, w_ref[:, cols], preferred_element_type=F32)
            if j in (BLK_UA, BLK_GB):
                pa_ref[rows, cols] = jax.nn.gelu(acc)
            elif j == BLK_VA:
                v = jax.nn.gelu(acc)
                vc = v - jnp.mean(v, axis=-1, keepdims=True)
                var = jnp.mean(vc * vc, axis=-1, keepdims=True)
                pa_ref[rows, cols] = vc * lax.rsqrt(var + EPS) * lng_ref[...] + lnb_ref[...]
            elif j in (BLK_XB, BLK_UC):
                pa_ref[rows, cols] = acc
            else:
                gcols = slice(cols.start - PA_COLS, cols.stop - PA_COLS)
                gate_ref[rows, gcols] = jax.nn.sigmoid(acc + bg_ref[:, gcols]).astype(BF16)


def _in_proj(x, norm_g, w_in, b_gate, ln_g, ln_b):
    r = x.shape[0]
    tm = min(r, 512)
    return pl.pallas_call(
        _in_proj_kernel,
        grid=(r // tm,),
        in_specs=[
            pl.BlockSpec((tm, D_MODEL), lambda i: (i, 0)),
            _resident((1, D_MODEL)),
            _single((D_MODEL, IN_COLS)),
            _resident((1, N_BRANCH * D_MODEL)),
            _resident((1, D_A)),
            _resident((1, D_A)),
        ],
        out_specs=[
            pl.BlockSpec((tm, PA_COLS), lambda i: (i, 0)),
            pl.BlockSpec((tm, N_BRANCH * D_MODEL), lambda i: (i, 0)),
        ],
        out_shape=[
            jax.ShapeDtypeStruct((r, PA_COLS), F32),
            jax.ShapeDtypeStruct((r, N_BRANCH * D_MODEL), BF16),
        ],
        compiler_params=_params(("parallel",)),
        name="in_proj",
    )(x, norm_g, w_in, b_gate, ln_g, ln_b)


def _mixer_a_mxu_kernel(u_ref, v_ref, wm_ref, bk_ref, o_ref, wk_scr, *, rows, n):
    @pl.when(pl.program_id(0) == 0)
    def _():
        shift = n.bit_length() - 1
        e_rows = (lax.broadcasted_iota(jnp.int32, (rows, CHUNK), 0) >> shift
                  == lax.broadcasted_iota(jnp.int32, (rows, CHUNK), 1)).astype(BF16)
        e_cols = (lax.broadcasted_iota(jnp.int32, (CHUNK, rows), 0)
                  == lax.broadcasted_iota(jnp.int32, (CHUNK, rows), 1) >> shift).astype(BF16)
        same_seq = ((lax.broadcasted_iota(jnp.int32, (rows, rows), 0) & (n - 1))
                    == (lax.broadcasted_iota(jnp.int32, (rows, rows), 1) & (n - 1)))
        for h in range(H_A):
            left = jnp.dot(e_rows, wm_ref[h], preferred_element_type=F32).astype(BF16)
            full = jnp.dot(left, e_cols, preferred_element_type=F32)
            wk_scr[h] = jnp.where(same_seq, full, 0.0).astype(BF16)

    vb = v_ref[...].astype(BF16)
    for h in range(H_A):
        cols = slice(h * HD_A, (h + 1) * HD_A)
        for r0 in range(0, rows, MXU_DIM):
            k = r0 + MXU_DIM
            mixed = jnp.dot(wk_scr[h, r0:k, 0:k], vb[0:k, cols], preferred_element_type=F32)
            o_ref[r0:k, cols] = (u_ref[r0:k, cols] * (mixed + bk_ref[r0:k, cols])).astype(BF16)


def _mixer_a_prompt(pa, t, n, wm, bk):
    rows = CHUNK * n
    assert rows % MXU_DIM == 0 and MXU_DIM % n == 0 and n & (n - 1) == 0
    return pl.pallas_call(
        functools.partial(_mixer_a_mxu_kernel, rows=rows, n=n),
        grid=(t // CHUNK,),
        in_specs=[
            pl.BlockSpec((rows, COL_BLK), lambda c: (c, BLK_UA)),
            pl.BlockSpec((rows, COL_BLK), lambda c: (c, BLK_VA)),
            _single((H_A, CHUNK, CHUNK)),
            _single((rows, D_A)),
        ],
        out_specs=pl.BlockSpec((rows, D_A), lambda c: (c, 0)),
        out_shape=jax.ShapeDtypeStruct((t * n, D_A), BF16),
        scratch_shapes=[pltpu.VMEM((H_A, rows, rows), BF16)],
        compiler_params=_params(("arbitrary",)),
        name="mixer_a_prompt",
    )(pa, pa, wm, bk)


def _mixer_a_vpu_kernel(u_ref, v_ref, wv_ref, bv_ref, o_ref, *, t, n):
    for i in range(t):
        mixed = wv_ref[i, 0] * v_ref[0]
        for s in range(1, i + 1):
            mixed = mixed + wv_ref[i, s] * v_ref[s]
        o_ref[i * n:(i + 1) * n] = (u_ref[i] * (mixed + bv_ref[i])).astype(BF16)


def _mixer_a_sample(proj, t, n, wv, bv):
    proj3 = proj.reshape(t, n, PA_COLS)
    return pl.pallas_call(
        functools.partial(_mixer_a_vpu_kernel, t=t, n=n),
        grid=(1,),
        in_specs=[
            pl.BlockSpec((t, n, COL_BLK), lambda i: (0, 0, BLK_UA)),
            pl.BlockSpec((t, n, COL_BLK), lambda i: (0, 0, BLK_VA)),
            _resident((t, t, 1, D_A)),
            _resident((t, 1, D_A)),
        ],
        out_specs=_resident((t * n, D_A)),
        out_shape=jax.ShapeDtypeStruct((t * n, D_A), BF16),
        compiler_params=_params(("arbitrary",)),
        name="mixer_a_sample",
    )(proj3, proj3, wv, bv)


def _lru_kernel(xb_ref, gb_ref, conv0_ref, h0_ref, cw_ref, cb_ref, wax_ref, ba_ref, bx_ref, lam_ref,
                yb_ref, convn_ref, hlast_ref, xp_scr, a_scr, d_scr, h_scr, *, tt, n):
    @pl.when(pl.program_id(0) == 0)
    def _():
        xp_scr[0:CONV_W - 1] = conv0_ref[...]
        h_scr[...] = h0_ref[...]

    xp_scr[CONV_W - 1:CONV_W - 1 + tt] = xb_ref[...]
    xc = cb_ref[...] + xp_scr[0:tt] * cw_ref[0]
    for k in range(1, CONV_W):
        xc = xc + xp_scr[k:k + tt] * cw_ref[k]
    tail = xp_scr[tt:tt + CONV_W - 1]
    convn_ref[...] = tail
    xp_scr[0:CONV_W - 1] = tail

    xc2 = xc.reshape(tt * n, D_B)
    pre = _bdot(xc2, wax_ref[...])
    r = jax.nn.sigmoid(pre[:, :D_B] + ba_ref[...])
    i = jax.nn.sigmoid(pre[:, D_B:] + bx_ref[...])
    log_a = -LRU_C * r * jax.nn.softplus(-lam_ref[...])
    a = jnp.exp(log_a)
    a_scr[...] = a.reshape(tt, n, D_B)
    gain = jnp.sqrt(-jnp.tanh(log_a) * (a * a + 1.0))
    d_scr[...] = (gain * (i * xc2)).reshape(tt, n, D_B)

    def step(s, h):
        h = a_scr[s] * h + d_scr[s]
        d_scr[s] = h
        return h

    h = lax.fori_loop(0, tt, step, h_scr[...], unroll=min(tt, 8))
    h_scr[...] = h
    hlast_ref[...] = h
    yb_ref[...] = (d_scr[...] * gb_ref[...]).reshape(tt * n, D_B).astype(BF16)


def _mixer_b(proj, t, n, conv0, h0, cw, cb, wax, ba, bx, lam):
    tt = min(t, 512 // n) if n <= 512 else 1
    proj3 = proj.reshape(t, n, PA_COLS)
    yb, conv_new, h_last = pl.pallas_call(
        functools.partial(_lru_kernel, tt=tt, n=n),
        grid=(t // tt,),
        in_specs=[
            pl.BlockSpec((tt, n, COL_BLK), lambda i: (i, 0, BLK_XB)),
            pl.BlockSpec((tt, n, COL_BLK), lambda i: (i, 0, BLK_GB)),
            _resident((CONV_W - 1, n, D_B)),
            _resident((n, D_B)),
            _resident((CONV_W, 1, D_B)),
            _resident((1, D_B)),
            _resident((D_B, 2 * D_B)),
            _resident((1, D_B)),
            _resident((1, D_B)),
            _resident((1, D_B)),
        ],
        out_specs=[
            pl.BlockSpec((tt * n, D_B), lambda i: (i, 0)),
            _resident((CONV_W - 1, n, D_B)),
            _resident((n, D_B)),
        ],
        out_shape=[
            jax.ShapeDtypeStruct((t * n, D_B), BF16),
            jax.ShapeDtypeStruct((CONV_W - 1, n, D_B), F32),
            jax.ShapeDtypeStruct((n, D_B), F32),
        ],
        scratch_shapes=[
            pltpu.VMEM((tt + CONV_W - 1, n, D_B), F32),
            pltpu.VMEM((tt, n, D_B), F32),
            pltpu.VMEM((tt, n, D_B), F32),
            pltpu.VMEM((n, D_B), F32),
        ],
        compiler_params=_params(("arbitrary",)),
        name="mixer_b",
    )(proj3, proj3, conv0, h0, cw, cb, wax, ba, bx, lam)
    return yb, conv_new, h_last


def _s5_disc_kernel(lr_ref, li_ref, ldt_ref, bre_ref, bim_ref, ar_ref, ai_ref, bbr_ref, bbi_ref):
    lr = lr_ref[...]
    li = li_ref[...]
    dt = jnp.exp(ldt_ref[...])
    mag = jnp.exp(lr * dt)
    ar = mag * jnp.cos(li * dt)
    ai = mag * jnp.sin(li * dt)
    ar_ref[...] = ar
    ai_ref[...] = ai
    den = lr * lr + li * li
    qr = ((ar - 1.0) * lr + ai * li) / den
    qi = (ai * lr - (ar - 1.0) * li) / den
    bre = bre_ref[...]
    bim = bim_ref[...]
    bbr_ref[...] = qr * bre - qi * bim
    bbi_ref[...] = qr * bim + qi * bre


def _s5_discretize(lam_re, lam_im, log_dt, b_re, b_im):
    g3 = jax.ShapeDtypeStruct((G_C, 1, P_STATE), F32)
    b3 = jax.ShapeDtypeStruct((G_C, HG_C, P_STATE), F32)
    return pl.pallas_call(
        _s5_disc_kernel,
        out_shape=[g3, g3, b3, b3],
        name="s5_discretize",
    )(lam_re.reshape(G_C, 1, P_STATE), lam_im.reshape(G_C, 1, P_STATE), log_dt.reshape(G_C, 1, 1),
      jnp.swapaxes(b_re, 1, 2), jnp.swapaxes(b_im, 1, 2))


def _s5_kernel(u_ref, s0_ref, ar_ref, ai_ref, bd_ref, cd_ref, d_ref, gw_ref, gb_ref,
               yc_ref, slast_ref, bu_scr, st_scr, *, tt, n):
    @pl.when(pl.program_id(0) == 0)
    def _():
        st_scr[...] = s0_ref[...]

    u2 = u_ref[...].reshape(tt * n, D_C)
    ub = u2.astype(BF16)
    for hf in range(S5_HALVES):
        bu = jnp.dot(ub[:, hf * S5_HALF_IN:(hf + 1) * S5_HALF_IN], bd_ref[hf], preferred_element_type=F32)
        bu_scr[:, :, hf * S5_HALF_ST:(hf + 1) * S5_HALF_ST] = bu.reshape(tt, n, S5_HALF_ST)

    lane_blk = 4 * LANE_TILE
    for hf in range(S5_HALVES):
        for q in range(S5_HALF_RI // lane_blk):
            lr = hf * S5_HALF_ST + q * lane_blk
            li = lr + S5_HALF_RI
            la = hf * S5_HALF_RI + q * lane_blk
            ar = jnp.broadcast_to(ar_ref[:, la:la + lane_blk], (SUBLANE_TILE, lane_blk))
            ai = jnp.broadcast_to(ai_ref[:, la:la + lane_blk], (SUBLANE_TILE, lane_blk))

            def rows_body(rg, carry, lr=lr, li=li, ar=ar, ai=ai):
                rs = pl.ds(pl.multiple_of(rg * SUBLANE_TILE, SUBLANE_TILE), SUBLANE_TILE)

                def step(s, x):
                    xr, xi = x
                    nr = ar * xr - ai * xi + bu_scr[s, rs, lr:lr + lane_blk]
                    ni = ar * xi + ai * xr + bu_scr[s, rs, li:li + lane_blk]
                    bu_scr[s, rs, lr:lr + lane_blk] = nr
                    bu_scr[s, rs, li:li + lane_blk] = ni
                    return nr, ni

                x0 = (st_scr[rs, lr:lr + lane_blk], st_scr[rs, li:li + lane_blk])
                xr, xi = lax.fori_loop(0, tt, step, x0, unroll=min(tt, 8))
                st_scr[rs, lr:lr + lane_blk] = xr
                st_scr[rs, li:li + lane_blk] = xi
                return carry

            lax.fori_loop(0, n // SUBLANE_TILE, rows_body, 0)

    slast_ref[...] = st_scr[...]
    ys = []
    for hf in range(S5_HALVES):
        xs = bu_scr[:, :, hf * S5_HALF_ST:(hf + 1) * S5_HALF_ST].reshape(tt * n, S5_HALF_ST)
        ys.append(_bdot(xs, cd_ref[hf]))
    y = jnp.concatenate(ys, axis=-1) + d_ref[...] * u2
    z = jax.nn.gelu(y)
    yc = z * jax.nn.sigmoid(_bdot(z, gw_ref[...]) + gb_ref[...])
    yc_ref[...] = yc.astype(BF16)


def _mixer_c(proj, t, n, s0, ar, ai, bd, cd, d, glu_w, glu_b):
    tt = min(t, 512 // n) if n <= 512 else 1
    proj3 = proj.reshape(t, n, PA_COLS)
    yc, s_last = pl.pallas_call(
        functools.partial(_s5_kernel, tt=tt, n=n),
        grid=(t // tt,),
        in_specs=[
            pl.BlockSpec((tt, n, COL_BLK), lambda i: (i, 0, BLK_UC)),
            _resident((n, S5_LANES)),
            _resident((1, S5_LANES // 2)),
            _resident((1, S5_LANES // 2)),
            _resident((S5_HALVES, S5_HALF_IN, S5_HALF_ST)),
            _resident((S5_HALVES, S5_HALF_ST, S5_HALF_IN)),
            _resident((1, D_C)),
            _resident((D_C, D_C)),
            _resident((1, D_C)),
        ],
        out_specs=[
            pl.BlockSpec((tt * n, D_C), lambda i: (i, 0)),
            _resident((n, S5_LANES)),
        ],
        out_shape=[
            jax.ShapeDtypeStruct((t * n, D_C), BF16),
            jax.ShapeDtypeStruct((n, S5_LANES), F32),
        ],
        scratch_shapes=[
            pltpu.VMEM((tt, n, S5_LANES), F32),
            pltpu.VMEM((n, S5_LANES), F32),
        ],
        compiler_params=_params(("arbitrary",)),
        name="mixer_c",
    )(proj3, s0, ar, ai, bd, cd, d, glu_w, glu_b)
    return yc, s_last


def _single(shape):
    zeros = (0,) * len(shape)
    return pl.BlockSpec(shape, lambda *_: zeros, pipeline_mode=pl.Buffered(1))


def _merge_kernel(ya_ref, yb_ref, yc_ref, g_ref, x_ref, wbr_ref, wout_ref, o_ref):
    m = None
    for k, y_ref in enumerate((ya_ref, yb_ref, yc_ref)):
        p = _bdot(y_ref[...], wbr_ref[k])
        g = g_ref[:, k * D_MODEL:(k + 1) * D_MODEL].astype(F32)
        m = g * p if m is None else m + g * p
    o_ref[...] = x_ref[...] + _bdot(m, wout_ref[...])


def _merge(ya, yb, yc, gates, x, wbr, wout):
    r = x.shape[0]
    tm = min(r, 512)
    row = pl.BlockSpec((tm, D_MODEL), lambda i: (i, 0))
    return pl.pallas_call(
        _merge_kernel,
        grid=(r // tm,),
        in_specs=[pl.BlockSpec((tm, D_BR), lambda i: (i, 0))] * N_BRANCH + [
            pl.BlockSpec((tm, N_BRANCH * D_MODEL), lambda i: (i, 0)),
            row,
            _single((N_BRANCH, D_BR, D_MODEL)),
            _single((D_MODEL, D_MODEL)),
        ],
        out_specs=row,
        out_shape=jax.ShapeDtypeStruct((r, D_MODEL), F32),
        compiler_params=_params(("parallel",)),
        name="merge",
    )(ya, yb, yc, gates, x, wbr, wout)


def _swiglu(hb, wg_ref, wu_ref, wd_ref):
    hidden = wg_ref.shape[-1]
    out = None
    for c0 in range(0, hidden, FFN_CHUNK):
        cols = slice(c0, min(c0 + FFN_CHUNK, hidden))
        gate = jnp.dot(hb, wg_ref[:, cols], preferred_element_type=F32)
        up = jnp.dot(hb, wu_ref[:, cols], preferred_element_type=F32)
        part = _bdot(jax.nn.silu(gate) * up, wd_ref[cols, :])
        out = part if out is None else out + part
    return out


def _dense_ffn_kernel(x_ref, nf_ref, wg_ref, wu_ref, wd_ref, fin_ref, o_ref, *, final):
    x1 = x_ref[...]
    hb = _rms(x1, nf_ref[...]).astype(BF16)
    x2 = x1 + _swiglu(hb, wg_ref, wu_ref, wd_ref)
    o_ref[...] = _rms(x2, fin_ref[...]) if final else x2


def _dense_ffn(x1, nf, wg, wu, wd, fin, final):
    r = x1.shape[0]
    tm = min(r, 1024)
    row = pl.BlockSpec((tm, D_MODEL), lambda i: (i, 0))
    return pl.pallas_call(
        functools.partial(_dense_ffn_kernel, final=final),
        grid=(r // tm,),
        in_specs=[
            row,
            _resident((1, D_MODEL)),
            _single((D_MODEL, D_FF)),
            _single((D_MODEL, D_FF)),
            _single((D_FF, D_MODEL)),
            _resident((1, D_MODEL)),
        ],
        out_specs=row,
        out_shape=jax.ShapeDtypeStruct((r, D_MODEL), F32),
        compiler_params=_params(("parallel",)),
        name="dense_ffn",
    )(x1, nf, wg, wu, wd, fin)


def _moe_kernel(x_ref, nf_ref, rw_ref, rb_ref, wg_ref, wu_ref, wd_ref, fin_ref, o_ref,
                hb_scr, comb_scr, acc_scr, *, final):
    e = pl.program_id(1)
    lane = lax.broadcasted_iota(jnp.int32, comb_scr.shape, 1)

    @pl.when(e == 0)
    def _():
        hn = _rms(x_ref[...], nf_ref[...])
        hb_scr[...] = hn.astype(BF16)
        hn_hi = hn.astype(BF16)
        hn_lo = (hn - hn_hi.astype(F32)).astype(BF16)
        rw = rw_ref[...]
        rw_hi = rw.astype(BF16)
        rw_lo = (rw - rw_hi.astype(F32)).astype(BF16)
        logits = (jnp.dot(hn_hi, rw_hi, preferred_element_type=F32)
                  + (jnp.dot(hn_lo, rw_hi, preferred_element_type=F32)
                     + jnp.dot(hn_hi, rw_lo, preferred_element_type=F32))) + rb_ref[...]
        neg = jnp.float32(-jnp.inf)
        lg = jnp.where(lane < N_EXPERTS, logits, neg)
        m1 = jnp.max(lg, axis=-1, keepdims=True)
        i1 = jnp.min(jnp.where(lg == m1, lane, LANE_TILE), axis=-1, keepdims=True)
        lg2 = jnp.where(lane == i1, neg, lg)
        m2 = jnp.max(lg2, axis=-1, keepdims=True)
        i2 = jnp.min(jnp.where(lg2 == m2, lane, LANE_TILE), axis=-1, keepdims=True)
        e2 = jnp.exp(m2 - m1)
        den = 1.0 + e2
        comb_scr[...] = jnp.where(lane == i1, 1.0 / den, 0.0) + jnp.where(lane == i2, e2 / den, 0.0)
        acc_scr[...] = jnp.zeros_like(acc_scr)

    out_e = _swiglu(hb_scr[...], wg_ref, wu_ref, wd_ref)
    w_e = jnp.sum(jnp.where(lane == e, comb_scr[...], 0.0), axis=-1, keepdims=True)
    acc_scr[...] += w_e * out_e

    @pl.when(e == N_EXPERTS - 1)
    def _():
        x2 = x_ref[...] + acc_scr[...]
        o_ref[...] = _rms(x2, fin_ref[...]) if final else x2


def _moe_ffn(x1, nf, rw, rb, wg, wu, wd, fin, final):
    r = x1.shape[0]
    tm = min(r, 1024)
    row = pl.BlockSpec((tm, D_MODEL), lambda i, e: (i, 0))
    return pl.pallas_call(
        functools.partial(_moe_kernel, final=final),
        grid=(r // tm, N_EXPERTS),
        in_specs=[
            row,
            _resident((1, D_MODEL)),
            _single((D_MODEL, LANE_TILE)),
            _resident((1, LANE_TILE)),
            pl.BlockSpec((None, D_MODEL, D_EXP), lambda i, e: (e, 0, 0)),
            pl.BlockSpec((None, D_MODEL, D_EXP), lambda i, e: (e, 0, 0)),
            pl.BlockSpec((None, D_EXP, D_MODEL), lambda i, e: (e, 0, 0)),
            _resident((1, D_MODEL)),
        ],
        out_specs=row,
        out_shape=jax.ShapeDtypeStruct((r, D_MODEL), F32),
        scratch_shapes=[
            pltpu.VMEM((tm, D_MODEL), BF16),
            pltpu.VMEM((tm, LANE_TILE), F32),
            pltpu.VMEM((tm, D_MODEL), F32),
        ],
        compiler_params=_params(("parallel", "arbitrary")),
        name="moe_ffn",
    )(x1, nf, rw, rb, wg, wu, wd, fin)


def _block_diag(w):
    h, i, j = w.shape
    eye = jnp.eye(h, dtype=w.dtype)
    return jnp.einsum("hij,hk->hikj", w, eye).reshape(h * i, h * j)


def _s5_in_matrix(bbr, bbi):
    gh = G_C // S5_HALVES
    out = []
    for part in (bbr, bbi):
        p4 = part.reshape(S5_HALVES, gh, HG_C, P_STATE)
        eye = jnp.eye(gh, dtype=part.dtype)
        out.append(jnp.einsum("aghp,gk->aghkp", p4, eye).reshape(S5_HALVES, gh * HG_C, gh * P_STATE))
    return jnp.concatenate(out, axis=-1).astype(BF16)


def _s5_out_matrix(c_re, c_im):
    gh = G_C // S5_HALVES
    out = []
    for part in (c_re, -c_im):
        p4 = part.reshape(S5_HALVES, gh, HG_C, P_STATE)
        eye = jnp.eye(gh, dtype=part.dtype)
        out.append(jnp.einsum("agop,gk->agpko", p4, eye).reshape(S5_HALVES, gh * P_STATE, gh * HG_C))
    return jnp.concatenate(out, axis=1).astype(BF16)


def _s5_state_in(re, im):
    n = re.shape[0]
    parts = [a.reshape(n, S5_HALVES, 1, S5_HALF_RI) for a in (re, im)]
    return jnp.concatenate(parts, axis=2).reshape(n, S5_LANES)


def _s5_state_out(s):
    n = s.shape[0]
    s4 = s.reshape(n, S5_HALVES, 2, S5_HALF_RI)
    return s4[:, :, 0].reshape(n, G_C, P_STATE), s4[:, :, 1].reshape(n, G_C, P_STATE)


def _lane_vec(v):
    return v.reshape(1, -1)


def _layer_weights(l, w):
    ar, ai, bbr, bbi = _s5_discretize(w["c_lam_re"][l], w["c_lam_im"][l], w["c_log_dt"][l],
                                      w["c_b_re"][l], w["c_b_im"][l])
    causal = jnp.tril(jnp.ones((CHUNK, CHUNK), dtype=bool))
    ws = w["a_ws"][l]
    lw = dict(
        norm_mix_g=_lane_vec(w["norm_mix_g"][l]),
        w_in=w["w_in"][l].astype(BF16),
        b_gate=_lane_vec(w["b_gate"][l]),
        a_ln_g=_lane_vec(w["a_ln_g"][l]),
        a_ln_b=_lane_vec(w["a_ln_b"][l]),
        a_wm=jnp.where(causal[None], ws, jnp.zeros_like(ws)),
        a_ws=ws,
        a_bs=w["a_bs"][l],
        b_cw=w["b_conv_w"][l][:, None, :],
        b_cb=_lane_vec(w["b_conv_b"][l]),
        b_wax=jnp.concatenate([_block_diag(w["b_wa"][l]), _block_diag(w["b_wx"][l])], axis=1).astype(BF16),
        b_ba=_lane_vec(w["b_ba"][l]),
        b_bx=_lane_vec(w["b_bx"][l]),
        b_lam=_lane_vec(w["b_lam"][l]),
        c_ar=ar.reshape(S5_HALVES, S5_HALF_RI).reshape(1, S5_LANES // 2),
        c_ai=ai.reshape(S5_HALVES, S5_HALF_RI).reshape(1, S5_LANES // 2),
        c_bd=_s5_in_matrix(bbr, bbi),
        c_cd=_s5_out_matrix(w["c_c_re"][l], w["c_c_im"][l]),
        c_d=_lane_vec(w["c_d"][l]),
        c_glu_w=w["c_glu_w"][l].astype(BF16),
        c_glu_b=_lane_vec(w["c_glu_b"][l]),
        w_branch=w["w_branch"][l].astype(BF16),
        w_out=w["w_out"][l].astype(BF16),
        norm_ffn_g=_lane_vec(w["norm_ffn_g"][l]),
        final_norm_g=_lane_vec(w["final_norm_g"]),
    )
    j = l // 2
    if l % 2 == 0:
        lw.update(ffn_wg=w["ffn_w_gate"][j].astype(BF16), ffn_wu=w["ffn_w_up"][j].astype(BF16),
                  ffn_wd=w["ffn_w_down"][j].astype(BF16))
    else:
        pad = LANE_TILE - N_EXPERTS
        lw.update(moe_rw=jnp.pad(w["moe_router_w"][j], ((0, 0), (0, pad))),
                  moe_rb=jnp.pad(w["moe_router_b"][j], (0, pad)).reshape(1, LANE_TILE),
                  moe_wg=w["moe_w_gate"][j].astype(BF16), moe_wu=w["moe_w_up"][j].astype(BF16),
                  moe_wd=w["moe_w_down"][j].astype(BF16))
    return lw


def _trunk(x_tm, t, n, conv_s, lru_s, s5re_s, s5im_s, layers):
    x = x_tm
    vas, convs, lrus, sres, sims = [], [], [], [], []
    for l, lw in enumerate(layers):
        proj, gates = _in_proj(x, lw["norm_mix_g"], lw["w_in"], lw["b_gate"], lw["a_ln_g"], lw["a_ln_b"])
        if t % CHUNK == 0:
            bk = jnp.broadcast_to(lw["a_bs"].T[:, None, :, None], (CHUNK, n, H_A, HD_A))
            ya = _mixer_a_prompt(proj, t, n, lw["a_wm"].astype(BF16), bk.reshape(CHUNK * n, D_A))
        else:
            assert t < CHUNK
            wv = jnp.repeat(jnp.transpose(lw["a_ws"][:, :t, :t], (1, 2, 0)), HD_A, axis=-1)[:, :, None, :]
            bv = jnp.repeat(jnp.transpose(lw["a_bs"][:, :t], (1, 0)), HD_A, axis=-1)[:, None, :]
            ya = _mixer_a_sample(proj, t, n, wv, bv)
        yb, conv_new, h_last = _mixer_b(proj, t, n, jnp.swapaxes(conv_s[l], 0, 1), lru_s[l], lw["b_cw"],
                                        lw["b_cb"], lw["b_wax"], lw["b_ba"], lw["b_bx"], lw["b_lam"])
        yc, s_last = _mixer_c(proj, t, n, _s5_state_in(s5re_s[l], s5im_s[l]), lw["c_ar"], lw["c_ai"],
                              lw["c_bd"], lw["c_cd"], lw["c_d"], lw["c_glu_w"], lw["c_glu_b"])
        final = l == len(layers) - 1
        x1 = _merge(ya, yb, yc, gates, x, lw["w_branch"], lw["w_out"])
        if l % 2 == 0:
            x = _dense_ffn(x1, lw["norm_ffn_g"], lw["ffn_wg"], lw["ffn_wu"], lw["ffn_wd"],
                           lw["final_norm_g"], final)
        else:
            x = _moe_ffn(x1, lw["norm_ffn_g"], lw["moe_rw"], lw["moe_rb"], lw["moe_wg"], lw["moe_wu"],
                         lw["moe_wd"], lw["final_norm_g"], final)
        va = proj.reshape(t, n, PA_COLS)[:, :, BLK_VA * COL_BLK:(BLK_VA + 1) * COL_BLK]
        vas.append(jnp.swapaxes(va, 0, 1))
        convs.append(jnp.swapaxes(conv_new, 0, 1))
        lrus.append(h_last)
        s_re, s_im = _s5_state_out(s_last)
        sres.append(s_re)
        sims.append(s_im)
    return x, jnp.stack(vas), jnp.stack(convs), jnp.stack(lrus), jnp.stack(sres), jnp.stack(sims)


def kernel(x_prompt, x_sample, state_conv_b, state_lru_b, state_s5_re, state_s5_im, norm_mix_g, w_in, b_gate, a_ln_g, a_ln_b, a_ws, a_bs, b_conv_w, b_conv_b, b_wa, b_ba, b_wx, b_bx, b_lam, c_lam_re, c_lam_im, c_log_dt, c_b_re, c_b_im, c_c_re, c_c_im, c_d, c_glu_w, c_glu_b, w_branch, w_out, norm_ffn_g, ffn_w_gate, ffn_w_up, ffn_w_down, moe_router_w, moe_router_b, moe_w_gate, moe_w_up, moe_w_down, final_norm_g):
    w = dict(norm_mix_g=norm_mix_g, w_in=w_in, b_gate=b_gate, a_ln_g=a_ln_g, a_ln_b=a_ln_b, a_ws=a_ws,
             a_bs=a_bs, b_conv_w=b_conv_w, b_conv_b=b_conv_b, b_wa=b_wa, b_ba=b_ba, b_wx=b_wx, b_bx=b_bx,
             b_lam=b_lam, c_lam_re=c_lam_re, c_lam_im=c_lam_im, c_log_dt=c_log_dt, c_b_re=c_b_re,
             c_b_im=c_b_im, c_c_re=c_c_re, c_c_im=c_c_im, c_d=c_d, c_glu_w=c_glu_w, c_glu_b=c_glu_b,
             w_branch=w_branch, w_out=w_out, norm_ffn_g=norm_ffn_g, ffn_w_gate=ffn_w_gate,
             ffn_w_up=ffn_w_up, ffn_w_down=ffn_w_down, moe_router_w=moe_router_w,
             moe_router_b=moe_router_b, moe_w_gate=moe_w_gate, moe_w_up=moe_w_up, moe_w_down=moe_w_down,
             final_norm_g=final_norm_g)
    layers = [_layer_weights(l, w) for l in range(DEPTH)]

    nb, tp, _ = x_prompt.shape
    ns, ts, _ = x_sample.shape
    conv0 = jnp.zeros((DEPTH, nb, CONV_W - 1, D_B), F32)
    lru0 = jnp.zeros((DEPTH, nb, D_B), F32)
    s50 = jnp.zeros((DEPTH, nb, G_C, P_STATE), F32)
    xp = jnp.swapaxes(x_prompt, 0, 1).reshape(tp * nb, D_MODEL)
    yp, _, conv_p, lru_p, sre_p, sim_p = _trunk(xp, tp, nb, conv0, lru0, s50, s50, layers)
    xs = jnp.swapaxes(x_sample, 0, 1).reshape(ts * ns, D_MODEL)
    ys, va_s, conv_s, lru_s, sre_s, sim_s = _trunk(xs, ts, ns, state_conv_b, state_lru_b, state_s5_re,
                                                   state_s5_im, layers)
    y_prompt = jnp.swapaxes(yp.reshape(tp, nb, D_MODEL), 0, 1)
    y_sample = jnp.swapaxes(ys.reshape(ts, ns, D_MODEL), 0, 1)
    return (y_prompt, y_sample, conv_p, lru_p, sre_p, sim_p, va_s, conv_s, lru_s, sre_s, sim_s)
```

```python
import functools
import math

import jax
import jax.numpy as jnp
from jax import lax
from jax.experimental import pallas as pl
from jax.experimental.pallas import tpu as pltpu

D_MODEL = 1024
DEPTH = 2
CHUNK = 128
D_A = D_MODEL // 2
H_A = 4
HD_A = D_A // H_A
D_B = D_MODEL // 2
H_B = 8
BH_B = D_B // H_B
CONV_W = 4
LRU_C = 8.0
D_C = D_MODEL // 2
HG_C = 16
G_C = D_C // HG_C
P_STATE = 64
N_BRANCH = 3
D_BR = D_MODEL // 2
IN_COLS = 2 * D_A + 2 * D_B + D_C + N_BRANCH * D_MODEL
D_FF = 11 * D_MODEL // 4
N_EXPERTS = 8
TOP_K = 2
D_EXP = D_FF // 2
EPS = 1e-6

F32 = jnp.float32
BF16 = jnp.bfloat16

COL_BLK = 512
N_COL_BLK = IN_COLS // COL_BLK
BLK_UA, BLK_VA, BLK_XB, BLK_GB, BLK_UC, BLK_GATE = 0, 1, 2, 3, 4, 5
PA_COLS = BLK_GATE * COL_BLK
S5_HALVES = 2
S5_HALF_IN = D_C // S5_HALVES
S5_HALF_RI = (G_C // S5_HALVES) * P_STATE
S5_HALF_ST = 2 * S5_HALF_RI
S5_LANES = S5_HALVES * S5_HALF_ST
LANE_TILE = 128
SUBLANE_TILE = 8
MXU_DIM = 256
FFN_CHUNK = 3 * MXU_DIM
VMEM_LIMIT = 56 * 1024 * 1024


def _resident(shape):
    zeros = (0,) * len(shape)
    return pl.BlockSpec(shape, lambda *_: zeros)


def _params(sem):
    return pltpu.CompilerParams(dimension_semantics=sem, vmem_limit_bytes=VMEM_LIMIT)


def _rms(x, g):
    return x * lax.rsqrt(jnp.mean(x * x, axis=-1, keepdims=True) + EPS) * g


def _bdot(a, b):
    return jnp.dot(a.astype(BF16), b, preferred_element_type=F32)


def _in_proj_kernel(x_ref, g_ref, w_ref, bg_ref, lng_ref, lnb_ref, pa_ref, gate_ref):
    half = x_ref.shape[0] // 2
    for r0 in (0, half):
        rows = slice(r0, r0 + half)
        hb = _rms(x_ref[rows, :], g_ref[...]).astype(BF16)
        for j in range(N_COL_BLK):
            cols = slice(j * COL_BLK, (j + 1) * COL_BLK)
            acc = jnp.dot(hb, w_ref[:, cols], preferred_element_type=F32)
            if j in (BLK_UA, BLK_GB):
                pa_ref[rows, cols] = jax.nn.gelu(acc)
            elif j == BLK_VA:
                v = jax.nn.gelu(acc)
                vc = v - jnp.mean(v, axis=-1, keepdims=True)
                var = jnp.mean(vc * vc, axis=-1, keepdims=True)
                pa_ref[rows, cols] = vc * lax.rsqrt(var + EPS) * lng_ref[...] + lnb_ref[...]
            elif j in (BLK_XB, BLK_UC):
                pa_ref[rows, cols] = acc
            else:
                gcols = slice(cols.start - PA_COLS, cols.stop - PA_COLS)
                gate_ref[rows, gcols] = jax.nn.sigmoid(acc + bg_ref[:, gcols]).astype(BF16)


def _in_proj(x, norm_g, w_in, b_gate, ln_g, ln_b):
    r = x.shape[0]
    tm = min(r, 512)
    return pl.pallas_call(
        _in_proj_kernel,
        grid=(r // tm,),
        in_specs=[
            pl.BlockSpec((tm, D_MODEL), lambda i: (i, 0)),
            _resident((1, D_MODEL)),
            _single((D_MODEL, IN_COLS)),
            _resident((1, N_BRANCH * D_MODEL)),
            _resident((1, D_A)),
            _resident((1, D_A)),
        ],
        out_specs=[
            pl.BlockSpec((tm, PA_COLS), lambda i: (i, 0)),
            pl.BlockSpec((tm, N_BRANCH * D_MODEL), lambda i: (i, 0)),
        ],
        out_shape=[
            jax.ShapeDtypeStruct((r, PA_COLS), F32),
            jax.ShapeDtypeStruct((r, N_BRANCH * D_MODEL), BF16),
        ],
        compiler_params=_params(("parallel",)),
        name="in_proj",
    )(x, norm_g, w_in, b_gate, ln_g, ln_b)


def _mixer_a_mxu_kernel(u_ref, v_ref, wm_ref, bk_ref, o_ref, wk_scr, *, rows, n):
    @pl.when(pl.program_id(0) == 0)
    def _():
        shift = n.bit_length() - 1
        e_rows = (lax.broadcasted_iota(jnp.int32, (rows, CHUNK), 0) >> shift
                  == lax.broadcasted_iota(jnp.int32, (rows, CHUNK), 1)).astype(BF16)
        e_cols = (lax.broadcasted_iota(jnp.int32, (CHUNK, rows), 0)
                  == lax.broadcasted_iota(jnp.int32, (CHUNK, rows), 1) >> shift).astype(BF16)
        same_seq = ((lax.broadcasted_iota(jnp.int32, (rows, rows), 0) & (n - 1))
                    == (lax.broadcasted_iota(jnp.int32, (rows, rows), 1) & (n - 1)))
        for h in range(H_A):
            left = jnp.dot(e_rows, wm_ref[h], preferred_element_type=F32).astype(BF16)
            full = jnp.dot(left, e_cols, preferred_element_type=F32)
            wk_scr[h] = jnp.where(same_seq, full, 0.0).astype(BF16)

    vb = v_ref[...].astype(BF16)
    for h in range(H_A):
        cols = slice(h * HD_A, (h + 1) * HD_A)
        for r0 in range(0, rows, MXU_DIM):
            k = r0 + MXU_DIM
            mixed = jnp.dot(wk_scr[h, r0:k, 0:k], vb[0:k, cols], preferred_element_type=F32)
            o_ref[r0:k, cols] = (u_ref[r0:k, cols] * (mixed + bk_ref[r0:k, cols])).astype(BF16)


def _mixer_a_prompt(pa, t, n, wm, bk):
    rows = CHUNK * n
    assert rows % MXU_DIM == 0 and MXU_DIM % n == 0 and n & (n - 1) == 0
    return pl.pallas_call(
        functools.partial(_mixer_a_mxu_kernel, rows=rows, n=n),
        grid=(t // CHUNK,),
        in_specs=[
            pl.BlockSpec((rows, COL_BLK), lambda c: (c, BLK_UA)),
            pl.BlockSpec((rows, COL_BLK), lambda c: (c, BLK_VA)),
            _single((H_A, CHUNK, CHUNK)),
            _single((rows, D_A)),
        ],
        out_specs=pl.BlockSpec((rows, D_A), lambda c: (c, 0)),
        out_shape=jax.ShapeDtypeStruct((t * n, D_A), BF16),
        scratch_shapes=[pltpu.VMEM((H_A, rows, rows), BF16)],
        compiler_params=_params(("arbitrary",)),
        name="mixer_a_prompt",
    )(pa, pa, wm, bk)


def _mixer_a_vpu_kernel(u_ref, v_ref, wv_ref, bv_ref, o_ref, *, t, n):
    for i in range(t):
        mixed = wv_ref[i, 0] * v_ref[0]
        for s in range(1, i + 1):
            mixed = mixed + wv_ref[i, s] * v_ref[s]
        o_ref[i * n:(i + 1) * n] = (u_ref[i] * (mixed + bv_ref[i])).astype(BF16)


def _mixer_a_sample(proj, t, n, wv, bv):
    proj3 = proj.reshape(t, n, PA_COLS)
    return pl.pallas_call(
        functools.partial(_mixer_a_vpu_kernel, t=t, n=n),
        grid=(1,),
        in_specs=[
            pl.BlockSpec((t, n, COL_BLK), lambda i: (0, 0, BLK_UA)),
            pl.BlockSpec((t, n, COL_BLK), lambda i: (0, 0, BLK_VA)),
            _resident((t, t, 1, D_A)),
            _resident((t, 1, D_A)),
        ],
        out_specs=_resident((t * n, D_A)),
        out_shape=jax.ShapeDtypeStruct((t * n, D_A), BF16),
        compiler_params=_params(("arbitrary",)),
        name="mixer_a_sample",
    )(proj3, proj3, wv, bv)


def _lru_kernel(xb_ref, gb_ref, conv0_ref, h0_ref, cw_ref, cb_ref, wax_ref, ba_ref, bx_ref, lam_ref,
                yb_ref, convn_ref, hlast_ref, xp_scr, a_scr, d_scr, h_scr, *, tt, n):
    @pl.when(pl.program_id(0) == 0)
    def _():
        xp_scr[0:CONV_W - 1] = conv0_ref[...]
        h_scr[...] = h0_ref[...]

    xp_scr[CONV_W - 1:CONV_W - 1 + tt] = xb_ref[...]
    xc = cb_ref[...] + xp_scr[0:tt] * cw_ref[0]
    for k in range(1, CONV_W):
        xc = xc + xp_scr[k:k + tt] * cw_ref[k]
    tail = xp_scr[tt:tt + CONV_W - 1]
    convn_ref[...] = tail
    xp_scr[0:CONV_W - 1] = tail

    xc2 = xc.reshape(tt * n, D_B)
    pre = _bdot(xc2, wax_ref[...])
    r = jax.nn.sigmoid(pre[:, :D_B] + ba_ref[...])
    i = jax.nn.sigmoid(pre[:, D_B:] + bx_ref[...])
    log_a = -LRU_C * r * jax.nn.softplus(-lam_ref[...])
    a = jnp.exp(log_a)
    a_scr[...] = a.reshape(tt, n, D_B)
    gain = jnp.sqrt(-jnp.tanh(log_a) * (a * a + 1.0))
    d_scr[...] = (gain * (i * xc2)).reshape(tt, n, D_B)

    def step(s, h):
        h = a_scr[s] * h + d_scr[s]
        d_scr[s] = h
        return h

    h = lax.fori_loop(0, tt, step, h_scr[...], unroll=min(tt, 8))
    h_scr[...] = h
    hlast_ref[...] = h
    yb_ref[...] = (d_scr[...] * gb_ref[...]).reshape(tt * n, D_B).astype(BF16)


def _mixer_b(proj, t, n, conv0, h0, cw, cb, wax, ba, bx, lam):
    tt = min(t, 512 // n) if n <= 512 else 1
    proj3 = proj.reshape(t, n, PA_COLS)
    yb, conv_new, h_last = pl.pallas_call(
        functools.partial(_lru_kernel, tt=tt, n=n),
        grid=(t // tt,),
        in_specs=[
            pl.BlockSpec((tt, n, COL_BLK), lambda i: (i, 0, BLK_XB)),
            pl.BlockSpec((tt, n, COL_BLK), lambda i: (i, 0, BLK_GB)),
            _resident((CONV_W - 1, n, D_B)),
            _resident((n, D_B)),
            _resident((CONV_W, 1, D_B)),
            _resident((1, D_B)),
            _resident((D_B, 2 * D_B)),
            _resident((1, D_B)),
            _resident((1, D_B)),
            _resident((1, D_B)),
        ],
        out_specs=[
            pl.BlockSpec((tt * n, D_B), lambda i: (i, 0)),
            _resident((CONV_W - 1, n, D_B)),
            _resident((n, D_B)),
        ],
        out_shape=[
            jax.ShapeDtypeStruct((t * n, D_B), BF16),
            jax.ShapeDtypeStruct((CONV_W - 1, n, D_B), F32),
            jax.ShapeDtypeStruct((n, D_B), F32),
        ],
        scratch_shapes=[
            pltpu.VMEM((tt + CONV_W - 1, n, D_B), F32),
            pltpu.VMEM((tt, n, D_B), F32),
            pltpu.VMEM((tt, n, D_B), F32),
            pltpu.VMEM((n, D_B), F32),
        ],
        compiler_params=_params(("arbitrary",)),
        name="mixer_b",
    )(proj3, proj3, conv0, h0, cw, cb, wax, ba, bx, lam)
    return yb, conv_new, h_last


def _s5_disc_kernel(lr_ref, li_ref, ldt_ref, bre_ref, bim_ref, ar_ref, ai_ref, bbr_ref, bbi_ref):
    lr = lr_ref[...]
    li = li_ref[...]
    dt = jnp.exp(ldt_ref[...])
    mag = jnp.exp(lr * dt)
    ar = mag * jnp.cos(li * dt)
    ai = mag * jnp.sin(li * dt)
    ar_ref[...] = ar
    ai_ref[...] = ai
    den = lr * lr + li * li
    qr = ((ar - 1.0) * lr + ai * li) / den
    qi = (ai * lr - (ar - 1.0) * li) / den
    bre = bre_ref[...]
    bim = bim_ref[...]
    bbr_ref[...] = qr * bre - qi * bim
    bbi_ref[...] = qr * bim + qi * bre


def _s5_discretize(lam_re, lam_im, log_dt, b_re, b_im):
    g3 = jax.ShapeDtypeStruct((G_C, 1, P_STATE), F32)
    b3 = jax.ShapeDtypeStruct((G_C, HG_C, P_STATE), F32)
    return pl.pallas_call(
        _s5_disc_kernel,
        out_shape=[g3, g3, b3, b3],
        name="s5_discretize",
    )(lam_re.reshape(G_C, 1, P_STATE), lam_im.reshape(G_C, 1, P_STATE), log_dt.reshape(G_C, 1, 1),
      jnp.swapaxes(b_re, 1, 2), jnp.swapaxes(b_im, 1, 2))


def _s5_kernel(u_ref, s0_ref, ar_ref, ai_ref, bd_ref, cd_ref, d_ref, gw_ref, gb_ref,
               yc_ref, slast_ref, bu_scr, st_scr, *, tt, n):
    @pl.when(pl.program_id(0) == 0)
    def _():
        st_scr[...] = s0_ref[...]

    u2 = u_ref[...].reshape(tt * n, D_C)
    ub = u2.astype(BF16)
    for hf in range(S5_HALVES):
        bu = jnp.dot(ub[:, hf * S5_HALF_IN:(hf + 1) * S5_HALF_IN], bd_ref[hf], preferred_element_type=F32)
        bu_scr[:, :, hf * S5_HALF_ST:(hf + 1) * S5_HALF_ST] = bu.reshape(tt, n, S5_HALF_ST)

    lane_blk = 4 * LANE_TILE
    for hf in range(S5_HALVES):
        for q in range(S5_HALF_RI // lane_blk):
            lr = hf * S5_HALF_ST + q * lane_blk
            li = lr + S5_HALF_RI
            la = hf * S5_HALF_RI + q * lane_blk
            ar = jnp.broadcast_to(ar_ref[:, la:la + lane_blk], (SUBLANE_TILE, lane_blk))
            ai = jnp.broadcast_to(ai_ref[:, la:la + lane_blk], (SUBLANE_TILE, lane_blk))

            def rows_body(rg, carry, lr=lr, li=li, ar=ar, ai=ai):
                rs = pl.ds(pl.multiple_of(rg * SUBLANE_TILE, SUBLANE_TILE), SUBLANE_TILE)

                def step(s, x):
                    xr, xi = x
                    nr = ar * xr - ai * xi + bu_scr[s, rs, lr:lr + lane_blk]
                    ni = ar * xi + ai * xr + bu_scr[s, rs, li:li + lane_blk]
                    bu_scr[s, rs, lr:lr + lane_blk] = nr
                    bu_scr[s, rs, li:li + lane_blk] = ni
                    return nr, ni

                x0 = (st_scr[rs, lr:lr + lane_blk], st_scr[rs, li:li + lane_blk])
                xr, xi = lax.fori_loop(0, tt, step, x0, unroll=min(tt, 8))
                st_scr[rs, lr:lr + lane_blk] = xr
                st_scr[rs, li:li + lane_blk] = xi
                return carry

            lax.fori_loop(0, n // SUBLANE_TILE, rows_body, 0)

    slast_ref[...] = st_scr[...]
    ys = []
    for hf in range(S5_HALVES):
        xs = bu_scr[:, :, hf * S5_HALF_ST:(hf + 1) * S5_HALF_ST].reshape(tt * n, S5_HALF_ST)
        ys.append(_bdot(xs, cd_ref[hf]))
    y = jnp.concatenate(ys, axis=-1) + d_ref[...] * u2
    z = jax.nn.gelu(y)
    yc = z * jax.nn.sigmoid(_bdot(z, gw_ref[...]) + gb_ref[...])
    yc_ref[...] = yc.astype(BF16)


def _mixer_c(proj, t, n, s0, ar, ai, bd, cd, d, glu_w, glu_b):
    tt = min(t, 512 // n) if n <= 512 else 1
    proj3 = proj.reshape(t, n, PA_COLS)
    yc, s_last = pl.pallas_call(
        functools.partial(_s5_kernel, tt=tt, n=n),
        grid=(t // tt,),
        in_specs=[
            pl.BlockSpec((tt, n, COL_BLK), lambda i: (i, 0, BLK_UC)),
            _resident((n, S5_LANES)),
            _resident((1, S5_LANES // 2)),
            _resident((1, S5_LANES // 2)),
            _resident((S5_HALVES, S5_HALF_IN, S5_HALF_ST)),
            _resident((S5_HALVES, S5_HALF_ST, S5_HALF_IN)),
            _resident((1, D_C)),
            _resident((D_C, D_C)),
            _resident((1, D_C)),
        ],
        out_specs=[
            pl.BlockSpec((tt * n, D_C), lambda i: (i, 0)),
            _resident((n, S5_LANES)),
        ],
        out_shape=[
            jax.ShapeDtypeStruct((t * n, D_C), BF16),
            jax.ShapeDtypeStruct((n, S5_LANES), F32),
        ],
        scratch_shapes=[
            pltpu.VMEM((tt, n, S5_LANES), F32),
            pltpu.VMEM((n, S5_LANES), F32),
        ],
        compiler_params=_params(("arbitrary",)),
        name="mixer_c",
    )(proj3, s0, ar, ai, bd, cd, d, glu_w, glu_b)
    return yc, s_last


def _single(shape):
    zeros = (0,) * len(shape)
    return pl.BlockSpec(shape, lambda *_: zeros, pipeline_mode=pl.Buffered(1))


def _merge_kernel(ya_ref, yb_ref, yc_ref, g_ref, x_ref, wbr_ref, wout_ref, o_ref):
    m = None
    for k, y_ref in enumerate((ya_ref, yb_ref, yc_ref)):
        p = _bdot(y_ref[...], wbr_ref[k])
        g = g_ref[:, k * D_MODEL:(k + 1) * D_MODEL].astype(F32)
        m = g * p if m is None else m + g * p
    o_ref[...] = x_ref[...] + _bdot(m, wout_ref[...])


def _merge(ya, yb, yc, gates, x, wbr, wout):
    r = x.shape[0]
    tm = min(r, 512)
    row = pl.BlockSpec((tm, D_MODEL), lambda i: (i, 0))
    return pl.pallas_call(
        _merge_kernel,
        grid=(r // tm,),
        in_specs=[pl.BlockSpec((tm, D_BR), lambda i: (i, 0))] * N_BRANCH + [
            pl.BlockSpec((tm, N_BRANCH * D_MODEL), lambda i: (i, 0)),
            row,
            _single((N_BRANCH, D_BR, D_MODEL)),
            _single((D_MODEL, D_MODEL)),
        ],
        out_specs=row,
        out_shape=jax.ShapeDtypeStruct((r, D_MODEL), F32),
        compiler_params=_params(("parallel",)),
        name="merge",
    )(ya, yb, yc, gates, x, wbr, wout)


def _swiglu(hb, wg_ref, wu_ref, wd_ref):
    hidden = wg_ref.shape[-1]
    out = None
    for c0 in range(0, hidden, FFN_CHUNK):
        cols = slice(c0, min(c0 + FFN_CHUNK, hidden))
        gate = jnp.dot(hb, wg_ref[:, cols], preferred_element_type=F32)
        up = jnp.dot(hb, wu_ref[:, cols], preferred_element_type=F32)
        part = _bdot(jax.nn.silu(gate) * up, wd_ref[cols, :])
        out = part if out is None else out + part
    return out


def _dense_ffn_kernel(x_ref, nf_ref, wg_ref, wu_ref, wd_ref, fin_ref, o_ref, *, final):
    x1 = x_ref[...]
    hb = _rms(x1, nf_ref[...]).astype(BF16)
    x2 = x1 + _swiglu(hb, wg_ref, wu_ref, wd_ref)
    o_ref[...] = _rms(x2, fin_ref[...]) if final else x2


def _dense_ffn(x1, nf, wg, wu, wd, fin, final):
    r = x1.shape[0]
    tm = min(r, 1024)
    row = pl.BlockSpec((tm, D_MODEL), lambda i: (i, 0))
    return pl.pallas_call(
        functools.partial(_dense_ffn_kernel, final=final),
        grid=(r // tm,),
        in_specs=[
            row,
            _resident((1, D_MODEL)),
            _single((D_MODEL, D_FF)),
            _single((D_MODEL, D_FF)),
            _single((D_FF, D_MODEL)),
            _resident((1, D_MODEL)),
        ],
        out_specs=row,
        out_shape=jax.ShapeDtypeStruct((r, D_MODEL), F32),
        compiler_params=_params(("parallel",)),
        name="dense_ffn",
    )(x1, nf, wg, wu, wd, fin)


def _moe_kernel(x_ref, nf_ref, rw_ref, rb_ref, wg_ref, wu_ref, wd_ref, fin_ref, o_ref,
                hb_scr, comb_scr, acc_scr, *, final):
    e = pl.program_id(1)
    lane = lax.broadcasted_iota(jnp.int32, comb_scr.shape, 1)

    @pl.when(e == 0)
    def _():
        hn = _rms(x_ref[...], nf_ref[...])
        hb_scr[...] = hn.astype(BF16)
        hn_hi = hn.astype(BF16)
        hn_lo = (hn - hn_hi.astype(F32)).astype(BF16)
        rw = rw_ref[...]
        rw_hi = rw.astype(BF16)
        rw_lo = (rw - rw_hi.astype(F32)).astype(BF16)
        logits = (jnp.dot(hn_hi, rw_hi, preferred_element_type=F32)
                  + (jnp.dot(hn_lo, rw_hi, preferred_element_type=F32)
                     + jnp.dot(hn_hi, rw_lo, preferred_element_type=F32))) + rb_ref[...]
        neg = jnp.float32(-jnp.inf)
        lg = jnp.where(lane < N_EXPERTS, logits, neg)
        m1 = jnp.max(lg, axis=-1, keepdims=True)
        i1 = jnp.min(jnp.where(lg == m1, lane, LANE_TILE), axis=-1, keepdims=True)
        lg2 = jnp.where(lane == i1, neg, lg)
        m2 = jnp.max(lg2, axis=-1, keepdims=True)
        i2 = jnp.min(jnp.where(lg2 == m2, lane, LANE_TILE), axis=-1, keepdims=True)
        e2 = jnp.exp(m2 - m1)
        den = 1.0 + e2
        comb_scr[...] = jnp.where(lane == i1, 1.0 / den, 0.0) + jnp.where(lane == i2, e2 / den, 0.0)
        acc_scr[...] = jnp.zeros_like(acc_scr)

    w_e = jnp.sum(jnp.where(lane == e, comb_scr[...], 0.0), axis=-1, keepdims=True)
    half = hb_scr.shape[0] // 2
    for r0 in (0, half):
        rows = slice(r0, r0 + half)
        acc_scr[rows, :] += w_e[rows] * _swiglu(hb_scr[rows, :], wg_ref, wu_ref, wd_ref)

    @pl.when(e == N_EXPERTS - 1)
    def _():
        x2 = x_ref[...] + acc_scr[...]
        o_ref[...] = _rms(x2, fin_ref[...]) if final else x2


def _moe_ffn(x1, nf, rw, rb, wg, wu, wd, fin, final):
    r = x1.shape[0]
    tm = min(r, 1024)
    row = pl.BlockSpec((tm, D_MODEL), lambda i, e: (i, 0))
    return pl.pallas_call(
        functools.partial(_moe_kernel, final=final),
        grid=(r // tm, N_EXPERTS),
        in_specs=[
            row,
            _resident((1, D_MODEL)),
            _single((D_MODEL, LANE_TILE)),
            _resident((1, LANE_TILE)),
            pl.BlockSpec((None, D_MODEL, D_EXP), lambda i, e: (e, 0, 0)),
            pl.BlockSpec((None, D_MODEL, D_EXP), lambda i, e: (e, 0, 0)),
            pl.BlockSpec((None, D_EXP, D_MODEL), lambda i, e: (e, 0, 0)),
            _resident((1, D_MODEL)),
        ],
        out_specs=row,
        out_shape=jax.ShapeDtypeStruct((r, D_MODEL), F32),
        scratch_shapes=[
            pltpu.VMEM((tm, D_MODEL), BF16),
            pltpu.VMEM((tm, LANE_TILE), F32),
            pltpu.VMEM((tm, D_MODEL), F32),
        ],
        compiler_params=_params(("parallel", "arbitrary")),
        name="moe_ffn",
    )(x1, nf, rw, rb, wg, wu, wd, fin)


def _block_diag(w):
    h, i, j = w.shape
    eye = jnp.eye(h, dtype=w.dtype)
    return jnp.einsum("hij,hk->hikj", w, eye).reshape(h * i, h * j)


def _s5_in_matrix(bbr, bbi):
    gh = G_C // S5_HALVES
    out = []
    for part in (bbr, bbi):
        p4 = part.reshape(S5_HALVES, gh, HG_C, P_STATE)
        eye = jnp.eye(gh, dtype=part.dtype)
        out.append(jnp.einsum("aghp,gk->aghkp", p4, eye).reshape(S5_HALVES, gh * HG_C, gh * P_STATE))
    return jnp.concatenate(out, axis=-1).astype(BF16)


def _s5_out_matrix(c_re, c_im):
    gh = G_C // S5_HALVES
    out = []
    for part in (c_re, -c_im):
        p4 = part.reshape(S5_HALVES, gh, HG_C, P_STATE)
        eye = jnp.eye(gh, dtype=part.dtype)
        out.append(jnp.einsum("agop,gk->agpko", p4, eye).reshape(S5_HALVES, gh * P_STATE, gh * HG_C))
    return jnp.concatenate(out, axis=1).astype(BF16)


def _s5_state_in(re, im):
    n = re.shape[0]
    parts = [a.reshape(n, S5_HALVES, 1, S5_HALF_RI) for a in (re, im)]
    return jnp.concatenate(parts, axis=2).reshape(n, S5_LANES)


def _s5_state_out(s):
    n = s.shape[0]
    s4 = s.reshape(n, S5_HALVES, 2, S5_HALF_RI)
    return s4[:, :, 0].reshape(n, G_C, P_STATE), s4[:, :, 1].reshape(n, G_C, P_STATE)


def _lane_vec(v):
    return v.reshape(1, -1)


def _layer_weights(l, w):
    ar, ai, bbr, bbi = _s5_discretize(w["c_lam_re"][l], w["c_lam_im"][l], w["c_log_dt"][l],
                                      w["c_b_re"][l], w["c_b_im"][l])
    causal = jnp.tril(jnp.ones((CHUNK, CHUNK), dtype=bool))
    ws = w["a_ws"][l]
    lw = dict(
        norm_mix_g=_lane_vec(w["norm_mix_g"][l]),
        w_in=w["w_in"][l].astype(BF16),
        b_gate=_lane_vec(w["b_gate"][l]),
        a_ln_g=_lane_vec(w["a_ln_g"][l]),
        a_ln_b=_lane_vec(w["a_ln_b"][l]),
        a_wm=jnp.where(causal[None], ws, jnp.zeros_like(ws)),
        a_ws=ws,
        a_bs=w["a_bs"][l],
        b_cw=w["b_conv_w"][l][:, None, :],
        b_cb=_lane_vec(w["b_conv_b"][l]),
        b_wax=jnp.concatenate([_block_diag(w["b_wa"][l]), _block_diag(w["b_wx"][l])], axis=1).astype(BF16),
        b_ba=_lane_vec(w["b_ba"][l]),
        b_bx=_lane_vec(w["b_bx"][l]),
        b_lam=_lane_vec(w["b_lam"][l]),
        c_ar=ar.reshape(S5_HALVES, S5_HALF_RI).reshape(1, S5_LANES // 2),
        c_ai=ai.reshape(S5_HALVES, S5_HALF_RI).reshape(1, S5_LANES // 2),
        c_bd=_s5_in_matrix(bbr, bbi),
        c_cd=_s5_out_matrix(w["c_c_re"][l], w["c_c_im"][l]),
        c_d=_lane_vec(w["c_d"][l]),
        c_glu_w=w["c_glu_w"][l].astype(BF16),
        c_glu_b=_lane_vec(w["c_glu_b"][l]),
        w_branch=w["w_branch"][l].astype(BF16),
        w_out=w["w_out"][l].astype(BF16),
        norm_ffn_g=_lane_vec(w["norm_ffn_g"][l]),
        final_norm_g=_lane_vec(w["final_norm_g"]),
    )
    j = l // 2
    if l % 2 == 0:
        lw.update(ffn_wg=w["ffn_w_gate"][j].astype(BF16), ffn_wu=w["ffn_w_up"][j].astype(BF16),
                  ffn_wd=w["ffn_w_down"][j].astype(BF16))
    else:
        pad = LANE_TILE - N_EXPERTS
        lw.update(moe_rw=jnp.pad(w["moe_router_w"][j], ((0, 0), (0, pad))),
                  moe_rb=jnp.pad(w["moe_router_b"][j], (0, pad)).reshape(1, LANE_TILE),
                  moe_wg=w["moe_w_gate"][j].astype(BF16), moe_wu=w["moe_w_up"][j].astype(BF16),
                  moe_wd=w["moe_w_down"][j].astype(BF16))
    return lw


def _trunk(x_tm, t, n, conv_s, lru_s, s5re_s, s5im_s, layers):
    x = x_tm
    vas, convs, lrus, sres, sims = [], [], [], [], []
    for l, lw in enumerate(layers):
        proj, gates = _in_proj(x, lw["norm_mix_g"], lw["w_in"], lw["b_gate"], lw["a_ln_g"], lw["a_ln_b"])
        if t % CHUNK == 0:
            bk = jnp.broadcast_to(lw["a_bs"].T[:, None, :, None], (CHUNK, n, H_A, HD_A))
            ya = _mixer_a_prompt(proj, t, n, lw["a_wm"].astype(BF16), bk.reshape(CHUNK * n, D_A))
        else:
            assert t < CHUNK
            wv = jnp.repeat(jnp.transpose(lw["a_ws"][:, :t, :t], (1, 2, 0)), HD_A, axis=-1)[:, :, None, :]
            bv = jnp.repeat(jnp.transpose(lw["a_bs"][:, :t], (1, 0)), HD_A, axis=-1)[:, None, :]
            ya = _mixer_a_sample(proj, t, n, wv, bv)
        yb, conv_new, h_last = _mixer_b(proj, t, n, jnp.swapaxes(conv_s[l], 0, 1), lru_s[l], lw["b_cw"],
                                        lw["b_cb"], lw["b_wax"], lw["b_ba"], lw["b_bx"], lw["b_lam"])
        yc, s_last = _mixer_c(proj, t, n, _s5_state_in(s5re_s[l], s5im_s[l]), lw["c_ar"], lw["c_ai"],
                              lw["c_bd"], lw["c_cd"], lw["c_d"], lw["c_glu_w"], lw["c_glu_b"])
        final = l == len(layers) - 1
        x1 = _merge(ya, yb, yc, gates, x, lw["w_branch"], lw["w_out"])
        if l % 2 == 0:
            x = _dense_ffn(x1, lw["norm_ffn_g"], lw["ffn_wg"], lw["ffn_wu"], lw["ffn_wd"],
                           lw["final_norm_g"], final)
        else:
            x = _moe_ffn(x1, lw["norm_ffn_g"], lw["moe_rw"], lw["moe_rb"], lw["moe_wg"], lw["moe_wu"],
                         lw["moe_wd"], lw["final_norm_g"], final)
        va = proj.reshape(t, n, PA_COLS)[:, :, BLK_VA * COL_BLK:(BLK_VA + 1) * COL_BLK]
        vas.append(jnp.swapaxes(va, 0, 1))
        convs.append(jnp.swapaxes(conv_new, 0, 1))
        lrus.append(h_last)
        s_re, s_im = _s5_state_out(s_last)
        sres.append(s_re)
        sims.append(s_im)
    return x, jnp.stack(vas), jnp.stack(convs), jnp.stack(lrus), jnp.stack(sres), jnp.stack(sims)


def kernel(x_prompt, x_sample, state_conv_b, state_lru_b, state_s5_re, state_s5_im, norm_mix_g, w_in, b_gate, a_ln_g, a_ln_b, a_ws, a_bs, b_conv_w, b_conv_b, b_wa, b_ba, b_wx, b_bx, b_lam, c_lam_re, c_lam_im, c_log_dt, c_b_re, c_b_im, c_c_re, c_c_im, c_d, c_glu_w, c_glu_b, w_branch, w_out, norm_ffn_g, ffn_w_gate, ffn_w_up, ffn_w_down, moe_router_w, moe_router_b, moe_w_gate, moe_w_up, moe_w_down, final_norm_g):
    w = dict(norm_mix_g=norm_mix_g, w_in=w_in, b_gate=b_gate, a_ln_g=a_ln_g, a_ln_b=a_ln_b, a_ws=a_ws,
             a_bs=a_bs, b_conv_w=b_conv_w, b_conv_b=b_conv_b, b_wa=b_wa, b_ba=b_ba, b_wx=b_wx, b_bx=b_bx,
             b_lam=b_lam, c_lam_re=c_lam_re, c_lam_im=c_lam_im, c_log_dt=c_log_dt, c_b_re=c_b_re,
             c_b_im=c_b_im, c_c_re=c_c_re, c_c_im=c_c_im, c_d=c_d, c_glu_w=c_glu_w, c_glu_b=c_glu_b,
             w_branch=w_branch, w_out=w_out, norm_ffn_g=norm_ffn_g, ffn_w_gate=ffn_w_gate,
             ffn_w_up=ffn_w_up, ffn_w_down=ffn_w_down, moe_router_w=moe_router_w,
             moe_router_b=moe_router_b, moe_w_gate=moe_w_gate, moe_w_up=moe_w_up, moe_w_down=moe_w_down,
             final_norm_g=final_norm_g)
    layers = [_layer_weights(l, w) for l in range(DEPTH)]

    nb, tp, _ = x_prompt.shape
    ns, ts, _ = x_sample.shape
    conv0 = jnp.zeros((DEPTH, nb, CONV_W - 1, D_B), F32)
    lru0 = jnp.zeros((DEPTH, nb, D_B), F32)
    s50 = jnp.zeros((DEPTH, nb, G_C, P_STATE), F32)
    xp = jnp.swapaxes(x_prompt, 0, 1).reshape(tp * nb, D_MODEL)
    yp, _, conv_p, lru_p, sre_p, sim_p = _trunk(xp, tp, nb, conv0, lru0, s50, s50, layers)
    xs = jnp.swapaxes(x_sample, 0, 1).reshape(ts * ns, D_MODEL)
    ys, va_s, conv_s, lru_s, sre_s, sim_s = _trunk(xs, ts, ns, state_conv_b, state_lru_b, state_s5_re,
                                                   state_s5_im, layers)
    y_prompt = jnp.swapaxes(yp.reshape(tp, nb, D_MODEL), 0, 1)
    y_sample = jnp.swapaxes(ys.reshape(ts, ns, D_MODEL), 0, 1)
    return (y_prompt, y_sample, conv_p, lru_p, sre_p, sim_p, va_s, conv_s, lru_s, sre_s, sim_s)
```

```python
import functools
import math

import jax
import jax.numpy as jnp
from jax import lax
from jax.experimental import pallas as pl
from jax.experimental.pallas import tpu as pltpu

D_MODEL = 1024
DEPTH = 2
CHUNK = 128
D_A = D_MODEL // 2
H_A = 4
HD_A = D_A // H_A
D_B = D_MODEL // 2
H_B = 8
BH_B = D_B // H_B
CONV_W = 4
LRU_C = 8.0
D_C = D_MODEL // 2
HG_C = 16
G_C = D_C // HG_C
P_STATE = 64
N_BRANCH = 3
D_BR = D_MODEL // 2
IN_COLS = 2 * D_A + 2 * D_B + D_C + N_BRANCH * D_MODEL
D_FF = 11 * D_MODEL // 4
N_EXPERTS = 8
TOP_K = 2
D_EXP = D_FF // 2
EPS = 1e-6

F32 = jnp.float32
BF16 = jnp.bfloat16

COL_BLK = 512
N_COL_BLK = IN_COLS // COL_BLK
BLK_UA, BLK_VA, BLK_XB, BLK_GB, BLK_UC, BLK_GATE = 0, 1, 2, 3, 4, 5
PA_COLS = BLK_GATE * COL_BLK
S5_HALVES = 2
S5_HALF_IN = D_C // S5_HALVES
S5_HALF_RI = (G_C // S5_HALVES) * P_STATE
S5_HALF_ST = 2 * S5_HALF_RI
S5_LANES = S5_HALVES * S5_HALF_ST
LANE_TILE = 128
SUBLANE_TILE = 8
MXU_DIM = 256
FFN_CHUNK = 3 * MXU_DIM
VMEM_LIMIT = 56 * 1024 * 1024


def _resident(shape):
    zeros = (0,) * len(shape)
    return pl.BlockSpec(shape, lambda *_: zeros)


def _params(sem):
    return pltpu.CompilerParams(dimension_semantics=sem, vmem_limit_bytes=VMEM_LIMIT)


def _rms(x, g):
    return x * lax.rsqrt(jnp.mean(x * x, axis=-1, keepdims=True) + EPS) * g


def _bdot(a, b):
    return jnp.dot(a.astype(BF16), b, preferred_element_type=F32)


def _in_proj_kernel(x_ref, g_ref, w_ref, bg_ref, lng_ref, lnb_ref, pa_ref, gate_ref):
    half = x_ref.shape[0] // 2
    for r0 in (0, half):
        rows = slice(r0, r0 + half)
        hb = _rms(x_ref[rows, :], g_ref[...]).astype(BF16)
        for j in range(N_COL_BLK):
            cols = slice(j * COL_BLK, (j + 1) * COL_BLK)
            acc = jnp.dot(hb, w_ref[:, cols], preferred_element_type=F32)
            if j in (BLK_UA, BLK_GB):
                pa_ref[rows, cols] = jax.nn.gelu(acc)
            elif j == BLK_VA:
                v = jax.nn.gelu(acc)
                vc = v - jnp.mean(v, axis=-1, keepdims=True)
                var = jnp.mean(vc * vc, axis=-1, keepdims=True)
                pa_ref[rows, cols] = vc * lax.rsqrt(var + EPS) * lng_ref[...] + lnb_ref[...]
            elif j in (BLK_XB, BLK_UC):
                pa_ref[rows, cols] = acc
            else:
                gcols = slice(cols.start - PA_COLS, cols.stop - PA_COLS)
                gate_ref[rows, gcols] = jax.nn.sigmoid(acc + bg_ref[:, gcols]).astype(BF16)


def _in_proj(x, norm_g, w_in, b_gate, ln_g, ln_b):
    r = x.shape[0]
    tm = min(r, 512)
    return pl.pallas_call(
        _in_proj_kernel,
        grid=(r // tm,),
        in_specs=[
            pl.BlockSpec((tm, D_MODEL), lambda i: (i, 0)),
            _resident((1, D_MODEL)),
            _single((D_MODEL, IN_COLS)),
            _resident((1, N_BRANCH * D_MODEL)),
            _resident((1, D_A)),
            _resident((1, D_A)),
        ],
        out_specs=[
            pl.BlockSpec((tm, PA_COLS), lambda i: (i, 0)),
            pl.BlockSpec((tm, N_BRANCH * D_MODEL), lambda i: (i, 0)),
        ],
        out_shape=[
            jax.ShapeDtypeStruct((r, PA_COLS), F32),
            jax.ShapeDtypeStruct((r, N_BRANCH * D_MODEL), BF16),
        ],
        compiler_params=_params(("parallel",)),
        name="in_proj",
    )(x, norm_g, w_in, b_gate, ln_g, ln_b)


def _mixer_a_mxu_kernel(u_ref, v_ref, wm_ref, bk_ref, o_ref, wk_scr, *, rows, n):
    @pl.when(pl.program_id(0) == 0)
    def _():
        shift = n.bit_length() - 1
        e_rows = (lax.broadcasted_iota(jnp.int32, (rows, CHUNK), 0) >> shift
                  == lax.broadcasted_iota(jnp.int32, (rows, CHUNK), 1)).astype(BF16)
        e_cols = (lax.broadcasted_iota(jnp.int32, (CHUNK, rows), 0)
                  == lax.broadcasted_iota(jnp.int32, (CHUNK, rows), 1) >> shift).astype(BF16)
        same_seq = ((lax.broadcasted_iota(jnp.int32, (rows, rows), 0) & (n - 1))
                    == (lax.broadcasted_iota(jnp.int32, (rows, rows), 1) & (n - 1)))
        for h in range(H_A):
            left = jnp.dot(e_rows, wm_ref[h], preferred_element_type=F32).astype(BF16)
            full = jnp.dot(left, e_cols, preferred_element_type=F32)
            wk_scr[h] = jnp.where(same_seq, full, 0.0).astype(BF16)

    vb = v_ref[...].astype(BF16)
    for h in range(H_A):
        cols = slice(h * HD_A, (h + 1) * HD_A)
        for r0 in range(0, rows, MXU_DIM):
            k = r0 + MXU_DIM
            mixed = jnp.dot(wk_scr[h, r0:k, 0:k], vb[0:k, cols], preferred_element_type=F32)
            o_ref[r0:k, cols] = (u_ref[r0:k, cols] * (mixed + bk_ref[r0:k, cols])).astype(BF16)


def _mixer_a_prompt(pa, t, n, wm, bk):
    rows = CHUNK * n
    assert rows % MXU_DIM == 0 and MXU_DIM % n == 0 and n & (n - 1) == 0
    return pl.pallas_call(
        functools.partial(_mixer_a_mxu_kernel, rows=rows, n=n),
        grid=(t // CHUNK,),
        in_specs=[
            pl.BlockSpec((rows, COL_BLK), lambda c: (c, BLK_UA)),
            pl.BlockSpec((rows, COL_BLK), lambda c: (c, BLK_VA)),
            _single((H_A, CHUNK, CHUNK)),
            _single((rows, D_A)),
        ],
        out_specs=pl.BlockSpec((rows, D_A), lambda c: (c, 0)),
        out_shape=jax.ShapeDtypeStruct((t * n, D_A), BF16),
        scratch_shapes=[pltpu.VMEM((H_A, rows, rows), BF16)],
        compiler_params=_params(("arbitrary",)),
        name="mixer_a_prompt",
    )(pa, pa, wm, bk)


def _mixer_a_vpu_kernel(u_ref, v_ref, wv_ref, bv_ref, o_ref, *, t, n):
    for i in range(t):
        mixed = wv_ref[i, 0] * v_ref[0]
        for s in range(1, i + 1):
            mixed = mixed + wv_ref[i, s] * v_ref[s]
        o_ref[i * n:(i + 1) * n] = (u_ref[i] * (mixed + bv_ref[i])).astype(BF16)


def _mixer_a_sample(proj, t, n, wv, bv):
    proj3 = proj.reshape(t, n, PA_COLS)
    return pl.pallas_call(
        functools.partial(_mixer_a_vpu_kernel, t=t, n=n),
        grid=(1,),
        in_specs=[
            pl.BlockSpec((t, n, COL_BLK), lambda i: (0, 0, BLK_UA)),
            pl.BlockSpec((t, n, COL_BLK), lambda i: (0, 0, BLK_VA)),
            _resident((t, t, 1, D_A)),
            _resident((t, 1, D_A)),
        ],
        out_specs=_resident((t * n, D_A)),
        out_shape=jax.ShapeDtypeStruct((t * n, D_A), BF16),
        compiler_params=_params(("arbitrary",)),
        name="mixer_a_sample",
    )(proj3, proj3, wv, bv)


def _lru_kernel(xb_ref, gb_ref, conv0_ref, h0_ref, cw_ref, cb_ref, wax_ref, ba_ref, bx_ref, lam_ref,
                yb_ref, convn_ref, hlast_ref, xp_scr, a_scr, d_scr, h_scr, *, tt, n):
    @pl.when(pl.program_id(0) == 0)
    def _():
        xp_scr[0:CONV_W - 1] = conv0_ref[...]
        h_scr[...] = h0_ref[...]

    xp_scr[CONV_W - 1:CONV_W - 1 + tt] = xb_ref[...]
    xc = cb_ref[...] + xp_scr[0:tt] * cw_ref[0]
    for k in range(1, CONV_W):
        xc = xc + xp_scr[k:k + tt] * cw_ref[k]
    tail = xp_scr[tt:tt + CONV_W - 1]
    convn_ref[...] = tail
    xp_scr[0:CONV_W - 1] = tail

    xc2 = xc.reshape(tt * n, D_B)
    pre = _bdot(xc2, wax_ref[...])
    r = jax.nn.sigmoid(pre[:, :D_B] + ba_ref[...])
    i = jax.nn.sigmoid(pre[:, D_B:] + bx_ref[...])
    log_a = -LRU_C * r * jax.nn.softplus(-lam_ref[...])
    a = jnp.exp(log_a)
    a_scr[...] = a.reshape(tt, n, D_B)
    gain = jnp.sqrt(-jnp.tanh(log_a) * (a * a + 1.0))
    d_scr[...] = (gain * (i * xc2)).reshape(tt, n, D_B)

    def step(s, h):
        h = a_scr[s] * h + d_scr[s]
        d_scr[s] = h
        return h

    h = lax.fori_loop(0, tt, step, h_scr[...], unroll=min(tt, 8))
    h_scr[...] = h
    hlast_ref[...] = h
    yb_ref[...] = (d_scr[...] * gb_ref[...]).reshape(tt * n, D_B).astype(BF16)


def _mixer_b(proj, t, n, conv0, h0, cw, cb, wax, ba, bx, lam):
    tt = min(t, 512 // n) if n <= 512 else 1
    proj3 = proj.reshape(t, n, PA_COLS)
    yb, conv_new, h_last = pl.pallas_call(
        functools.partial(_lru_kernel, tt=tt, n=n),
        grid=(t // tt,),
        in_specs=[
            pl.BlockSpec((tt, n, COL_BLK), lambda i: (i, 0, BLK_XB)),
            pl.BlockSpec((tt, n, COL_BLK), lambda i: (i, 0, BLK_GB)),
            _resident((CONV_W - 1, n, D_B)),
            _resident((n, D_B)),
            _resident((CONV_W, 1, D_B)),
            _resident((1, D_B)),
            _resident((D_B, 2 * D_B)),
            _resident((1, D_B)),
            _resident((1, D_B)),
            _resident((1, D_B)),
        ],
        out_specs=[
            pl.BlockSpec((tt * n, D_B), lambda i: (i, 0)),
            _resident((CONV_W - 1, n, D_B)),
            _resident((n, D_B)),
        ],
        out_shape=[
            jax.ShapeDtypeStruct((t * n, D_B), BF16),
            jax.ShapeDtypeStruct((CONV_W - 1, n, D_B), F32),
            jax.ShapeDtypeStruct((n, D_B), F32),
        ],
        scratch_shapes=[
            pltpu.VMEM((tt + CONV_W - 1, n, D_B), F32),
            pltpu.VMEM((tt, n, D_B), F32),
            pltpu.VMEM((tt, n, D_B), F32),
            pltpu.VMEM((n, D_B), F32),
        ],
        compiler_params=_params(("arbitrary",)),
        name="mixer_b",
    )(proj3, proj3, conv0, h0, cw, cb, wax, ba, bx, lam)
    return yb, conv_new, h_last


def _s5_disc_kernel(lr_ref, li_ref, ldt_ref, bre_ref, bim_ref, ar_ref, ai_ref, bbr_ref, bbi_ref):
    lr = lr_ref[...]
    li = li_ref[...]
    dt = jnp.exp(ldt_ref[...])
    mag = jnp.exp(lr * dt)
    ar = mag * jnp.cos(li * dt)
    ai = mag * jnp.sin(li * dt)
    ar_ref[...] = ar
    ai_ref[...] = ai
    den = lr * lr + li * li
    qr = ((ar - 1.0) * lr + ai * li) / den
    qi = (ai * lr - (ar - 1.0) * li) / den
    bre = bre_ref[...]
    bim = bim_ref[...]
    bbr_ref[...] = qr * bre - qi * bim
    bbi_ref[...] = qr * bim + qi * bre


def _s5_discretize(lam_re, lam_im, log_dt, b_re, b_im):
    g3 = jax.ShapeDtypeStruct((G_C, 1, P_STATE), F32)
    b3 = jax.ShapeDtypeStruct((G_C, HG_C, P_STATE), F32)
    return pl.pallas_call(
        _s5_disc_kernel,
        out_shape=[g3, g3, b3, b3],
        name="s5_discretize",
    )(lam_re.reshape(G_C, 1, P_STATE), lam_im.reshape(G_C, 1, P_STATE), log_dt.reshape(G_C, 1, 1),
      jnp.swapaxes(b_re, 1, 2), jnp.swapaxes(b_im, 1, 2))


def _s5_kernel(u_ref, s0_ref, ar_ref, ai_ref, bd_ref, cd_ref, d_ref, gw_ref, gb_ref,
               yc_ref, slast_ref, bu_scr, st_scr, *, tt, n):
    @pl.when(pl.program_id(0) == 0)
    def _():
        st_scr[...] = s0_ref[...]

    u2 = u_ref[...].reshape(tt * n, D_C)
    ub = u2.astype(BF16)
    for hf in range(S5_HALVES):
        bu = jnp.dot(ub[:, hf * S5_HALF_IN:(hf + 1) * S5_HALF_IN], bd_ref[hf], preferred_element_type=F32)
        bu_scr[:, :, hf * S5_HALF_ST:(hf + 1) * S5_HALF_ST] = bu.reshape(tt, n, S5_HALF_ST)

    lane_blk = 4 * LANE_TILE
    for hf in range(S5_HALVES):
        for q in range(S5_HALF_RI // lane_blk):
            lr = hf * S5_HALF_ST + q * lane_blk
            li = lr + S5_HALF_RI
            la = hf * S5_HALF_RI + q * lane_blk
            ar = jnp.broadcast_to(ar_ref[:, la:la + lane_blk], (SUBLANE_TILE, lane_blk))
            ai = jnp.broadcast_to(ai_ref[:, la:la + lane_blk], (SUBLANE_TILE, lane_blk))

            def rows_body(rg, carry, lr=lr, li=li, ar=ar, ai=ai):
                rs = pl.ds(pl.multiple_of(rg * SUBLANE_TILE, SUBLANE_TILE), SUBLANE_TILE)

                def step(s, x):
                    xr, xi = x
                    nr = ar * xr - ai * xi + bu_scr[s, rs, lr:lr + lane_blk]
                    ni = ar * xi + ai * xr + bu_scr[s, rs, li:li + lane_blk]
                    bu_scr[s, rs, lr:lr + lane_blk] = nr
                    bu_scr[s, rs, li:li + lane_blk] = ni
                    return nr, ni

                x0 = (st_scr[rs, lr:lr + lane_blk], st_scr[rs, li:li + lane_blk])
                xr, xi = lax.fori_loop(0, tt, step, x0, unroll=min(tt, 8))
                st_scr[rs, lr:lr + lane_blk] = xr
                st_scr[rs, li:li + lane_blk] = xi
                return carry

            if n == SUBLANE_TILE:
                xr, xi = st_scr[:, lr:lr + lane_blk], st_scr[:, li:li + lane_blk]
                for s in range(tt):
                    nr = ar * xr - ai * xi + bu_scr[s, :, lr:lr + lane_blk]
                    ni = ar * xi + ai * xr + bu_scr[s, :, li:li + lane_blk]
                    bu_scr[s, :, lr:lr + lane_blk] = nr
                    bu_scr[s, :, li:li + lane_blk] = ni
                    xr, xi = nr, ni
                st_scr[:, lr:lr + lane_blk] = xr
                st_scr[:, li:li + lane_blk] = xi
            else:
                lax.fori_loop(0, n // SUBLANE_TILE, rows_body, 0)

    slast_ref[...] = st_scr[...]
    ys = []
    for hf in range(S5_HALVES):
        xs = bu_scr[:, :, hf * S5_HALF_ST:(hf + 1) * S5_HALF_ST].reshape(tt * n, S5_HALF_ST)
        ys.append(_bdot(xs, cd_ref[hf]))
    y = jnp.concatenate(ys, axis=-1) + d_ref[...] * u2
    z = jax.nn.gelu(y)
    yc = z * jax.nn.sigmoid(_bdot(z, gw_ref[...]) + gb_ref[...])
    yc_ref[...] = yc.astype(BF16)


def _mixer_c(proj, t, n, s0, ar, ai, bd, cd, d, glu_w, glu_b):
    tt = min(t, 512 // n) if n <= 512 else 1
    proj3 = proj.reshape(t, n, PA_COLS)
    yc, s_last = pl.pallas_call(
        functools.partial(_s5_kernel, tt=tt, n=n),
        grid=(t // tt,),
        in_specs=[
            pl.BlockSpec((tt, n, COL_BLK), lambda i: (i, 0, BLK_UC)),
            _resident((n, S5_LANES)),
            _resident((1, S5_LANES // 2)),
            _resident((1, S5_LANES // 2)),
            _resident((S5_HALVES, S5_HALF_IN, S5_HALF_ST)),
            _resident((S5_HALVES, S5_HALF_ST, S5_HALF_IN)),
            _resident((1, D_C)),
            _resident((D_C, D_C)),
            _resident((1, D_C)),
        ],
        out_specs=[
            pl.BlockSpec((tt * n, D_C), lambda i: (i, 0)),
            _resident((n, S5_LANES)),
        ],
        out_shape=[
            jax.ShapeDtypeStruct((t * n, D_C), BF16),
            jax.ShapeDtypeStruct((n, S5_LANES), F32),
        ],
        scratch_shapes=[
            pltpu.VMEM((tt, n, S5_LANES), F32),
            pltpu.VMEM((n, S5_LANES), F32),
        ],
        compiler_params=_params(("arbitrary",)),
        name="mixer_c",
    )(proj3, s0, ar, ai, bd, cd, d, glu_w, glu_b)
    return yc, s_last


def _single(shape):
    zeros = (0,) * len(shape)
    return pl.BlockSpec(shape, lambda *_: zeros, pipeline_mode=pl.Buffered(1))


def _merge_kernel(ya_ref, yb_ref, yc_ref, g_ref, x_ref, wbr_ref, wout_ref, o_ref):
    m = None
    for k, y_ref in enumerate((ya_ref, yb_ref, yc_ref)):
        p = _bdot(y_ref[...], wbr_ref[k])
        g = g_ref[:, k * D_MODEL:(k + 1) * D_MODEL].astype(F32)
        m = g * p if m is None else m + g * p
    o_ref[...] = x_ref[...] + _bdot(m, wout_ref[...])


def _merge(ya, yb, yc, gates, x, wbr, wout):
    r = x.shape[0]
    tm = min(r, 512)
    row = pl.BlockSpec((tm, D_MODEL), lambda i: (i, 0))
    return pl.pallas_call(
        _merge_kernel,
        grid=(r // tm,),
        in_specs=[pl.BlockSpec((tm, D_BR), lambda i: (i, 0))] * N_BRANCH + [
            pl.BlockSpec((tm, N_BRANCH * D_MODEL), lambda i: (i, 0)),
            row,
            _single((N_BRANCH, D_BR, D_MODEL)),
            _single((D_MODEL, D_MODEL)),
        ],
        out_specs=row,
        out_shape=jax.ShapeDtypeStruct((r, D_MODEL), F32),
        compiler_params=_params(("parallel",)),
        name="merge",
    )(ya, yb, yc, gates, x, wbr, wout)


def _swiglu(hb, wg_ref, wu_ref, wd_ref):
    hidden = wg_ref.shape[-1]
    out = None
    for c0 in range(0, hidden, FFN_CHUNK):
        cols = slice(c0, min(c0 + FFN_CHUNK, hidden))
        gate = jnp.dot(hb, wg_ref[:, cols], preferred_element_type=F32)
        up = jnp.dot(hb, wu_ref[:, cols], preferred_element_type=F32)
        part = _bdot(jax.nn.silu(gate) * up, wd_ref[cols, :])
        out = part if out is None else out + part
    return out


def _dense_ffn_kernel(x_ref, nf_ref, wg_ref, wu_ref, wd_ref, fin_ref, o_ref, *, final):
    x1 = x_ref[...]
    hb = _rms(x1, nf_ref[...]).astype(BF16)
    x2 = x1 + _swiglu(hb, wg_ref, wu_ref, wd_ref)
    o_ref[...] = _rms(x2, fin_ref[...]) if final else x2


def _dense_ffn(x1, nf, wg, wu, wd, fin, final):
    r = x1.shape[0]
    tm = min(r, 1024)
    row = pl.BlockSpec((tm, D_MODEL), lambda i: (i, 0))
    return pl.pallas_call(
        functools.partial(_dense_ffn_kernel, final=final),
        grid=(r // tm,),
        in_specs=[
            row,
            _resident((1, D_MODEL)),
            _single((D_MODEL, D_FF)),
            _single((D_MODEL, D_FF)),
            _single((D_FF, D_MODEL)),
            _resident((1, D_MODEL)),
        ],
        out_specs=row,
        out_shape=jax.ShapeDtypeStruct((r, D_MODEL), F32),
        compiler_params=_params(("parallel",)),
        name="dense_ffn",
    )(x1, nf, wg, wu, wd, fin)


def _moe_kernel(x_ref, nf_ref, rw_ref, rb_ref, wg_ref, wu_ref, wd_ref, fin_ref, o_ref,
                hb_scr, comb_scr, acc_scr, *, final):
    e = pl.program_id(1)
    lane = lax.broadcasted_iota(jnp.int32, comb_scr.shape, 1)

    @pl.when(e == 0)
    def _():
        hn = _rms(x_ref[...], nf_ref[...])
        hb_scr[...] = hn.astype(BF16)
        hn_hi = hn.astype(BF16)
        hn_lo = (hn - hn_hi.astype(F32)).astype(BF16)
        rw = rw_ref[...]
        rw_hi = rw.astype(BF16)
        rw_lo = (rw - rw_hi.astype(F32)).astype(BF16)
        logits = (jnp.dot(hn_hi, rw_hi, preferred_element_type=F32)
                  + (jnp.dot(hn_lo, rw_hi, preferred_element_type=F32)
                     + jnp.dot(hn_hi, rw_lo, preferred_element_type=F32))) + rb_ref[...]
        neg = jnp.float32(-jnp.inf)
        lg = jnp.where(lane < N_EXPERTS, logits, neg)
        m1 = jnp.max(lg, axis=-1, keepdims=True)
        i1 = jnp.min(jnp.where(lg == m1, lane, LANE_TILE), axis=-1, keepdims=True)
        lg2 = jnp.where(lane == i1, neg, lg)
        m2 = jnp.max(lg2, axis=-1, keepdims=True)
        i2 = jnp.min(jnp.where(lg2 == m2, lane, LANE_TILE), axis=-1, keepdims=True)
        e2 = jnp.exp(m2 - m1)
        den = 1.0 + e2
        comb_scr[...] = jnp.where(lane == i1, 1.0 / den, 0.0) + jnp.where(lane == i2, e2 / den, 0.0)
        acc_scr[...] = jnp.zeros_like(acc_scr)

    out_e = _swiglu(hb_scr[...], wg_ref, wu_ref, wd_ref)
    w_e = jnp.sum(jnp.where(lane == e, comb_scr[...], 0.0), axis=-1, keepdims=True)
    acc_scr[...] += w_e * out_e

    @pl.when(e == N_EXPERTS - 1)
    def _():
        x2 = x_ref[...] + acc_scr[...]
        o_ref[...] = _rms(x2, fin_ref[...]) if final else x2


def _moe_ffn(x1, nf, rw, rb, wg, wu, wd, fin, final):
    r = x1.shape[0]
    tm = min(r, 1024)
    row = pl.BlockSpec((tm, D_MODEL), lambda i, e: (i, 0))
    return pl.pallas_call(
        functools.partial(_moe_kernel, final=final),
        grid=(r // tm, N_EXPERTS),
        in_specs=[
            row,
            _resident((1, D_MODEL)),
            _single((D_MODEL, LANE_TILE)),
            _resident((1, LANE_TILE)),
            pl.BlockSpec((None, D_MODEL, D_EXP), lambda i, e: (e, 0, 0)),
            pl.BlockSpec((None, D_MODEL, D_EXP), lambda i, e: (e, 0, 0)),
            pl.BlockSpec((None, D_EXP, D_MODEL), lambda i, e: (e, 0, 0)),
            _resident((1, D_MODEL)),
        ],
        out_specs=row,
        out_shape=jax.ShapeDtypeStruct((r, D_MODEL), F32),
        scratch_shapes=[
            pltpu.VMEM((tm, D_MODEL), BF16),
            pltpu.VMEM((tm, LANE_TILE), F32),
            pltpu.VMEM((tm, D_MODEL), F32),
        ],
        compiler_params=_params(("parallel", "arbitrary")),
        name="moe_ffn",
    )(x1, nf, rw, rb, wg, wu, wd, fin)


def _block_diag(w):
    h, i, j = w.shape
    eye = jnp.eye(h, dtype=w.dtype)
    return jnp.einsum("hij,hk->hikj", w, eye).reshape(h * i, h * j)


def _s5_in_matrix(bbr, bbi):
    gh = G_C // S5_HALVES
    out = []
    for part in (bbr, bbi):
        p4 = part.reshape(S5_HALVES, gh, HG_C, P_STATE)
        eye = jnp.eye(gh, dtype=part.dtype)
        out.append(jnp.einsum("aghp,gk->aghkp", p4, eye).reshape(S5_HALVES, gh * HG_C, gh * P_STATE))
    return jnp.concatenate(out, axis=-1).astype(BF16)


def _s5_out_matrix(c_re, c_im):
    gh = G_C // S5_HALVES
    out = []
    for part in (c_re, -c_im):
        p4 = part.reshape(S5_HALVES, gh, HG_C, P_STATE)
        eye = jnp.eye(gh, dtype=part.dtype)
        out.append(jnp.einsum("agop,gk->agpko", p4, eye).reshape(S5_HALVES, gh * P_STATE, gh * HG_C))
    return jnp.concatenate(out, axis=1).astype(BF16)


def _s5_state_in(re, im):
    n = re.shape[0]
    parts = [a.reshape(n, S5_HALVES, 1, S5_HALF_RI) for a in (re, im)]
    return jnp.concatenate(parts, axis=2).reshape(n, S5_LANES)


def _s5_state_out(s):
    n = s.shape[0]
    s4 = s.reshape(n, S5_HALVES, 2, S5_HALF_RI)
    return s4[:, :, 0].reshape(n, G_C, P_STATE), s4[:, :, 1].reshape(n, G_C, P_STATE)


def _lane_vec(v):
    return v.reshape(1, -1)


def _layer_weights(l, w):
    ar, ai, bbr, bbi = _s5_discretize(w["c_lam_re"][l], w["c_lam_im"][l], w["c_log_dt"][l],
                                      w["c_b_re"][l], w["c_b_im"][l])
    causal = jnp.tril(jnp.ones((CHUNK, CHUNK), dtype=bool))
    ws = w["a_ws"][l]
    lw = dict(
        norm_mix_g=_lane_vec(w["norm_mix_g"][l]),
        w_in=w["w_in"][l].astype(BF16),
        b_gate=_lane_vec(w["b_gate"][l]),
        a_ln_g=_lane_vec(w["a_ln_g"][l]),
        a_ln_b=_lane_vec(w["a_ln_b"][l]),
        a_wm=jnp.where(causal[None], ws, jnp.zeros_like(ws)),
        a_ws=ws,
        a_bs=w["a_bs"][l],
        b_cw=w["b_conv_w"][l][:, None, :],
        b_cb=_lane_vec(w["b_conv_b"][l]),
        b_wax=jnp.concatenate([_block_diag(w["b_wa"][l]), _block_diag(w["b_wx"][l])], axis=1).astype(BF16),
        b_ba=_lane_vec(w["b_ba"][l]),
        b_bx=_lane_vec(w["b_bx"][l]),
        b_lam=_lane_vec(w["b_lam"][l]),
        c_ar=ar.reshape(S5_HALVES, S5_HALF_RI).reshape(1, S5_LANES // 2),
        c_ai=ai.reshape(S5_HALVES, S5_HALF_RI).reshape(1, S5_LANES // 2),
        c_bd=_s5_in_matrix(bbr, bbi),
        c_cd=_s5_out_matrix(w["c_c_re"][l], w["c_c_im"][l]),
        c_d=_lane_vec(w["c_d"][l]),
        c_glu_w=w["c_glu_w"][l].astype(BF16),
        c_glu_b=_lane_vec(w["c_glu_b"][l]),
        w_branch=w["w_branch"][l].astype(BF16),
        w_out=w["w_out"][l].astype(BF16),
        norm_ffn_g=_lane_vec(w["norm_ffn_g"][l]),
        final_norm_g=_lane_vec(w["final_norm_g"]),
    )
    j = l // 2
    if l % 2 == 0:
        lw.update(ffn_wg=w["ffn_w_gate"][j].astype(BF16), ffn_wu=w["ffn_w_up"][j].astype(BF16),
                  ffn_wd=w["ffn_w_down"][j].astype(BF16))
    else:
        pad = LANE_TILE - N_EXPERTS
        lw.update(moe_rw=jnp.pad(w["moe_router_w"][j], ((0, 0), (0, pad))),
                  moe_rb=jnp.pad(w["moe_router_b"][j], (0, pad)).reshape(1, LANE_TILE),
                  moe_wg=w["moe_w_gate"][j].astype(BF16), moe_wu=w["moe_w_up"][j].astype(BF16),
                  moe_wd=w["moe_w_down"][j].astype(BF16))
    return lw


def _trunk(x_tm, t, n, conv_s, lru_s, s5re_s, s5im_s, layers):
    x = x_tm
    vas, convs, lrus, sres, sims = [], [], [], [], []
    for l, lw in enumerate(layers):
        proj, gates = _in_proj(x, lw["norm_mix_g"], lw["w_in"], lw["b_gate"], lw["a_ln_g"], lw["a_ln_b"])
        if t % CHUNK == 0:
            bk = jnp.broadcast_to(lw["a_bs"].T[:, None, :, None], (CHUNK, n, H_A, HD_A))
            ya = _mixer_a_prompt(proj, t, n, lw["a_wm"].astype(BF16), bk.reshape(CHUNK * n, D_A))
        else:
            assert t < CHUNK
            wv = jnp.repeat(jnp.transpose(lw["a_ws"][:, :t, :t], (1, 2, 0)), HD_A, axis=-1)[:, :, None, :]
            bv = jnp.repeat(jnp.transpose(lw["a_bs"][:, :t], (1, 0)), HD_A, axis=-1)[:, None, :]
            ya = _mixer_a_sample(proj, t, n, wv, bv)
        yb, conv_new, h_last = _mixer_b(proj, t, n, jnp.swapaxes(conv_s[l], 0, 1), lru_s[l], lw["b_cw"],
                                        lw["b_cb"], lw["b_wax"], lw["b_ba"], lw["b_bx"], lw["b_lam"])
        yc, s_last = _mixer_c(proj, t, n, _s5_state_in(s5re_s[l], s5im_s[l]), lw["c_ar"], lw["c_ai"],
                              lw["c_bd"], lw["c_cd"], lw["c_d"], lw["c_glu_w"], lw["c_glu_b"])
        final = l == len(layers) - 1
        x1 = _merge(ya, yb, yc, gates, x, lw["w_branch"], lw["w_out"])
        if l % 2 == 0:
            x = _dense_ffn(x1, lw["norm_ffn_g"], lw["ffn_wg"], lw["ffn_wu"], lw["ffn_wd"],
                           lw["final_norm_g"], final)
        else:
            x = _moe_ffn(x1, lw["norm_ffn_g"], lw["moe_rw"], lw["moe_rb"], lw["moe_wg"], lw["moe_wu"],
                         lw["moe_wd"], lw["final_norm_g"], final)
        va = proj.reshape(t, n, PA_COLS)[:, :, BLK_VA * COL_BLK:(BLK_VA + 1) * COL_BLK]
        vas.append(jnp.swapaxes(va, 0, 1))
        convs.append(jnp.swapaxes(conv_new, 0, 1))
        lrus.append(h_last)
        s_re, s_im = _s5_state_out(s_last)
        sres.append(s_re)
        sims.append(s_im)
    return x, jnp.stack(vas), jnp.stack(convs), jnp.stack(lrus), jnp.stack(sres), jnp.stack(sims)


def kernel(x_prompt, x_sample, state_conv_b, state_lru_b, state_s5_re, state_s5_im, norm_mix_g, w_in, b_gate, a_ln_g, a_ln_b, a_ws, a_bs, b_conv_w, b_conv_b, b_wa, b_ba, b_wx, b_bx, b_lam, c_lam_re, c_lam_im, c_log_dt, c_b_re, c_b_im, c_c_re, c_c_im, c_d, c_glu_w, c_glu_b, w_branch, w_out, norm_ffn_g, ffn_w_gate, ffn_w_up, ffn_w_down, moe_router_w, moe_router_b, moe_w_gate, moe_w_up, moe_w_down, final_norm_g):
    w = dict(norm_mix_g=norm_mix_g, w_in=w_in, b_gate=b_gate, a_ln_g=a_ln_g, a_ln_b=a_ln_b, a_ws=a_ws,
             a_bs=a_bs, b_conv_w=b_conv_w, b_conv_b=b_conv_b, b_wa=b_wa, b_ba=b_ba, b_wx=b_wx, b_bx=b_bx,
             b_lam=b_lam, c_lam_re=c_lam_re, c_lam_im=c_lam_im, c_log_dt=c_log_dt, c_b_re=c_b_re,
             c_b_im=c_b_im, c_c_re=c_c_re, c_c_im=c_c_im, c_d=c_d, c_glu_w=c_glu_w, c_glu_b=c_glu_b,
             w_branch=w_branch, w_out=w_out, norm_ffn_g=norm_ffn_g, ffn_w_gate=ffn_w_gate,
             ffn_w_up=ffn_w_up, ffn_w_down=ffn_w_down, moe_router_w=moe_router_w,
             moe_router_b=moe_router_b, moe_w_gate=moe_w_gate, moe_w_up=moe_w_up, moe_w_down=moe_w_down,
             final_norm_g=final_norm_g)
    layers = [_layer_weights(l, w) for l in range(DEPTH)]

    nb, tp, _ = x_prompt.shape
    ns, ts, _ = x_sample.shape
    conv0 = jnp.zeros((DEPTH, nb, CONV_W - 1, D_B), F32)
    lru0 = jnp.zeros((DEPTH, nb, D_B), F32)
    s50 = jnp.zeros((DEPTH, nb, G_C, P_STATE), F32)
    xp = jnp.swapaxes(x_prompt, 0, 1).reshape(tp * nb, D_MODEL)
    yp, _, conv_p, lru_p, sre_p, sim_p = _trunk(xp, tp, nb, conv0, lru0, s50, s50, layers)
    xs = jnp.swapaxes(x_sample, 0, 1).reshape(ts * ns, D_MODEL)
    ys, va_s, conv_s, lru_s, sre_s, sim_s = _trunk(xs, ts, ns, state_conv_b, state_lru_b, state_s5_re,
                                                   state_s5_im, layers)
    y_prompt = jnp.swapaxes(yp.reshape(tp, nb, D_MODEL), 0, 1)
    y_sample = jnp.swapaxes(ys.reshape(ts, ns, D_MODEL), 0, 1)
    return (y_prompt, y_sample, conv_p, lru_p, sre_p, sim_p, va_s, conv_s, lru_s, sre_s, sim_s)
```

```python
import functools
import math

import jax
import jax.numpy as jnp
from jax import lax
from jax.experimental import pallas as pl
from jax.experimental.pallas import tpu as pltpu

D_MODEL = 1024
DEPTH = 2
CHUNK = 128
D_A = D_MODEL // 2
H_A = 4
HD_A = D_A // H_A
D_B = D_MODEL // 2
H_B = 8
BH_B = D_B // H_B
CONV_W = 4
LRU_C = 8.0
D_C = D_MODEL // 2
HG_C = 16
G_C = D_C // HG_C
P_STATE = 64
N_BRANCH = 3
D_BR = D_MODEL // 2
IN_COLS = 2 * D_A + 2 * D_B + D_C + N_BRANCH * D_MODEL
D_FF = 11 * D_MODEL // 4
N_EXPERTS = 8
TOP_K = 2
D_EXP = D_FF // 2
EPS = 1e-6

F32 = jnp.float32
BF16 = jnp.bfloat16

COL_BLK = 512
N_COL_BLK = IN_COLS // COL_BLK
BLK_UA, BLK_VA, BLK_XB, BLK_GB, BLK_UC, BLK_GATE = 0, 1, 2, 3, 4, 5
PA_COLS = BLK_GATE * COL_BLK
S5_HALVES = 2
S5_HALF_IN = D_C // S5_HALVES
S5_HALF_RI = (G_C // S5_HALVES) * P_STATE
S5_HALF_ST = 2 * S5_HALF_RI
S5_LANES = S5_HALVES * S5_HALF_ST
LANE_TILE = 128
SUBLANE_TILE = 8
MXU_DIM = 256
FFN_CHUNK = 3 * MXU_DIM
VMEM_LIMIT = 56 * 1024 * 1024


def _resident(shape):
    zeros = (0,) * len(shape)
    return pl.BlockSpec(shape, lambda *_: zeros)


def _params(sem, fuse_inputs=None):
    return pltpu.CompilerParams(dimension_semantics=sem, vmem_limit_bytes=VMEM_LIMIT,
                                allow_input_fusion=fuse_inputs)


def _rms(x, g):
    return x * lax.rsqrt(jnp.mean(x * x, axis=-1, keepdims=True) + EPS) * g


def _bdot(a, b):
    return jnp.dot(a.astype(BF16), b, preferred_element_type=F32)


def _in_proj_kernel(x_ref, g_ref, w_ref, bg_ref, lng_ref, lnb_ref, pa_ref, gate_ref):
    half = x_ref.shape[0] // 2
    for r0 in (0, half):
        rows = slice(r0, r0 + half)
        hb = _rms(x_ref[rows, :], g_ref[...]).astype(BF16)
        for j in range(N_COL_BLK):
            cols = slice(j * COL_BLK, (j + 1) * COL_BLK)
            acc = jnp.dot(hb, w_ref[:, cols], preferred_element_type=F32)
            if j in (BLK_UA, BLK_GB):
                pa_ref[rows, cols] = jax.nn.gelu(acc)
            elif j == BLK_VA:
                v = jax.nn.gelu(acc)
                vc = v - jnp.mean(v, axis=-1, keepdims=True)
                var = jnp.mean(vc * vc, axis=-1, keepdims=True)
                pa_ref[rows, cols] = vc * lax.rsqrt(var + EPS) * lng_ref[...] + lnb_ref[...]
            elif j in (BLK_XB, BLK_UC):
                pa_ref[rows, cols] = acc
            else:
                gcols = slice(cols.start - PA_COLS, cols.stop - PA_COLS)
                gate_ref[rows, gcols] = jax.nn.sigmoid(acc + bg_ref[:, gcols]).astype(BF16)


def _in_proj(x, norm_g, w_in, b_gate, ln_g, ln_b):
    r = x.shape[0]
    tm = min(r, 512)
    return pl.pallas_call(
        _in_proj_kernel,
        grid=(r // tm,),
        in_specs=[
            pl.BlockSpec((tm, D_MODEL), lambda i: (i, 0)),
            _resident((1, D_MODEL)),
            _single((D_MODEL, IN_COLS)),
            _resident((1, N_BRANCH * D_MODEL)),
            _resident((1, D_A)),
            _resident((1, D_A)),
        ],
        out_specs=[
            pl.BlockSpec((tm, PA_COLS), lambda i: (i, 0)),
            pl.BlockSpec((tm, N_BRANCH * D_MODEL), lambda i: (i, 0)),
        ],
        out_shape=[
            jax.ShapeDtypeStruct((r, PA_COLS), F32),
            jax.ShapeDtypeStruct((r, N_BRANCH * D_MODEL), BF16),
        ],
        compiler_params=_params(("parallel",), fuse_inputs=[True] + [False] * 5),
        name="in_proj",
    )(x, norm_g, w_in, b_gate, ln_g, ln_b)


def _mixer_a_mxu_kernel(u_ref, v_ref, wm_ref, bk_ref, o_ref, wk_scr, *, rows, n):
    @pl.when(pl.program_id(0) == 0)
    def _():
        shift = n.bit_length() - 1
        e_rows = (lax.broadcasted_iota(jnp.int32, (rows, CHUNK), 0) >> shift
                  == lax.broadcasted_iota(jnp.int32, (rows, CHUNK), 1)).astype(BF16)
        e_cols = (lax.broadcasted_iota(jnp.int32, (CHUNK, rows), 0)
                  == lax.broadcasted_iota(jnp.int32, (CHUNK, rows), 1) >> shift).astype(BF16)
        same_seq = ((lax.broadcasted_iota(jnp.int32, (rows, rows), 0) & (n - 1))
                    == (lax.broadcasted_iota(jnp.int32, (rows, rows), 1) & (n - 1)))
        for h in range(H_A):
            left = jnp.dot(e_rows, wm_ref[h], preferred_element_type=F32).astype(BF16)
            full = jnp.dot(left, e_cols, preferred_element_type=F32)
            wk_scr[h] = jnp.where(same_seq, full, 0.0).astype(BF16)

    vb = v_ref[...].astype(BF16)
    for h in range(H_A):
        cols = slice(h * HD_A, (h + 1) * HD_A)
        for r0 in range(0, rows, MXU_DIM):
            k = r0 + MXU_DIM
            mixed = jnp.dot(wk_scr[h, r0:k, 0:k], vb[0:k, cols], preferred_element_type=F32)
            o_ref[r0:k, cols] = (u_ref[r0:k, cols] * (mixed + bk_ref[r0:k, cols])).astype(BF16)


def _mixer_a_prompt(pa, t, n, wm, bk):
    rows = CHUNK * n
    assert rows % MXU_DIM == 0 and MXU_DIM % n == 0 and n & (n - 1) == 0
    return pl.pallas_call(
        functools.partial(_mixer_a_mxu_kernel, rows=rows, n=n),
        grid=(t // CHUNK,),
        in_specs=[
            pl.BlockSpec((rows, COL_BLK), lambda c: (c, BLK_UA)),
            pl.BlockSpec((rows, COL_BLK), lambda c: (c, BLK_VA)),
            _single((H_A, CHUNK, CHUNK)),
            _single((rows, D_A)),
        ],
        out_specs=pl.BlockSpec((rows, D_A), lambda c: (c, 0)),
        out_shape=jax.ShapeDtypeStruct((t * n, D_A), BF16),
        scratch_shapes=[pltpu.VMEM((H_A, rows, rows), BF16)],
        compiler_params=_params(("arbitrary",)),
        name="mixer_a_prompt",
    )(pa, pa, wm, bk)


def _mixer_a_vpu_kernel(u_ref, v_ref, wv_ref, bv_ref, o_ref, *, t, n):
    for i in range(t):
        mixed = wv_ref[i, 0] * v_ref[0]
        for s in range(1, i + 1):
            mixed = mixed + wv_ref[i, s] * v_ref[s]
        o_ref[i * n:(i + 1) * n] = (u_ref[i] * (mixed + bv_ref[i])).astype(BF16)


def _mixer_a_sample(proj, t, n, wv, bv):
    proj3 = proj.reshape(t, n, PA_COLS)
    return pl.pallas_call(
        functools.partial(_mixer_a_vpu_kernel, t=t, n=n),
        grid=(1,),
        in_specs=[
            pl.BlockSpec((t, n, COL_BLK), lambda i: (0, 0, BLK_UA)),
            pl.BlockSpec((t, n, COL_BLK), lambda i: (0, 0, BLK_VA)),
            _resident((t, t, 1, D_A)),
            _resident((t, 1, D_A)),
        ],
        out_specs=_resident((t * n, D_A)),
        out_shape=jax.ShapeDtypeStruct((t * n, D_A), BF16),
        compiler_params=_params(("arbitrary",)),
        name="mixer_a_sample",
    )(proj3, proj3, wv, bv)


def _lru_kernel(xb_ref, gb_ref, conv0_ref, h0_ref, cw_ref, cb_ref, wax_ref, ba_ref, bx_ref, lam_ref,
                yb_ref, convn_ref, hlast_ref, xp_scr, a_scr, d_scr, h_scr, *, tt, n):
    @pl.when(pl.program_id(0) == 0)
    def _():
        xp_scr[0:CONV_W - 1] = conv0_ref[...]
        h_scr[...] = h0_ref[...]

    xp_scr[CONV_W - 1:CONV_W - 1 + tt] = xb_ref[...]
    xc = cb_ref[...] + xp_scr[0:tt] * cw_ref[0]
    for k in range(1, CONV_W):
        xc = xc + xp_scr[k:k + tt] * cw_ref[k]
    tail = xp_scr[tt:tt + CONV_W - 1]
    convn_ref[...] = tail
    xp_scr[0:CONV_W - 1] = tail

    xc2 = xc.reshape(tt * n, D_B)
    pre = _bdot(xc2, wax_ref[...])
    r = jax.nn.sigmoid(pre[:, :D_B] + ba_ref[...])
    i = jax.nn.sigmoid(pre[:, D_B:] + bx_ref[...])
    log_a = -LRU_C * r * jax.nn.softplus(-lam_ref[...])
    a = jnp.exp(log_a)
    a_scr[...] = a.reshape(tt, n, D_B)
    gain = jnp.sqrt(-jnp.tanh(log_a) * (a * a + 1.0))
    d_scr[...] = (gain * (i * xc2)).reshape(tt, n, D_B)

    def step(s, h):
        h = a_scr[s] * h + d_scr[s]
        d_scr[s] = h
        return h

    h = lax.fori_loop(0, tt, step, h_scr[...], unroll=min(tt, 8))
    h_scr[...] = h
    hlast_ref[...] = h
    yb_ref[...] = (d_scr[...] * gb_ref[...]).reshape(tt * n, D_B).astype(BF16)


def _mixer_b(proj, t, n, conv0, h0, cw, cb, wax, ba, bx, lam):
    tt = min(t, 512 // n) if n <= 512 else 1
    proj3 = proj.reshape(t, n, PA_COLS)
    yb, conv_new, h_last = pl.pallas_call(
        functools.partial(_lru_kernel, tt=tt, n=n),
        grid=(t // tt,),
        in_specs=[
            pl.BlockSpec((tt, n, COL_BLK), lambda i: (i, 0, BLK_XB)),
            pl.BlockSpec((tt, n, COL_BLK), lambda i: (i, 0, BLK_GB)),
            _resident((CONV_W - 1, n, D_B)),
            _resident((n, D_B)),
            _resident((CONV_W, 1, D_B)),
            _resident((1, D_B)),
            _resident((D_B, 2 * D_B)),
            _resident((1, D_B)),
            _resident((1, D_B)),
            _resident((1, D_B)),
        ],
        out_specs=[
            pl.BlockSpec((tt * n, D_B), lambda i: (i, 0)),
            _resident((CONV_W - 1, n, D_B)),
            _resident((n, D_B)),
        ],
        out_shape=[
            jax.ShapeDtypeStruct((t * n, D_B), BF16),
            jax.ShapeDtypeStruct((CONV_W - 1, n, D_B), F32),
            jax.ShapeDtypeStruct((n, D_B), F32),
        ],
        scratch_shapes=[
            pltpu.VMEM((tt + CONV_W - 1, n, D_B), F32),
            pltpu.VMEM((tt, n, D_B), F32),
            pltpu.VMEM((tt, n, D_B), F32),
            pltpu.VMEM((n, D_B), F32),
        ],
        compiler_params=_params(("arbitrary",)),
        name="mixer_b",
    )(proj3, proj3, conv0, h0, cw, cb, wax, ba, bx, lam)
    return yb, conv_new, h_last


def _s5_disc_kernel(lr_ref, li_ref, ldt_ref, bre_ref, bim_ref, ar_ref, ai_ref, bbr_ref, bbi_ref):
    lr = lr_ref[...]
    li = li_ref[...]
    dt = jnp.exp(ldt_ref[...])
    mag = jnp.exp(lr * dt)
    ar = mag * jnp.cos(li * dt)
    ai = mag * jnp.sin(li * dt)
    ar_ref[...] = ar
    ai_ref[...] = ai
    den = lr * lr + li * li
    qr = ((ar - 1.0) * lr + ai * li) / den
    qi = (ai * lr - (ar - 1.0) * li) / den
    bre = bre_ref[...]
    bim = bim_ref[...]
    bbr_ref[...] = qr * bre - qi * bim
    bbi_ref[...] = qr * bim + qi * bre


def _s5_discretize(lam_re, lam_im, log_dt, b_re, b_im):
    g3 = jax.ShapeDtypeStruct((G_C, 1, P_STATE), F32)
    b3 = jax.ShapeDtypeStruct((G_C, HG_C, P_STATE), F32)
    return pl.pallas_call(
        _s5_disc_kernel,
        out_shape=[g3, g3, b3, b3],
        name="s5_discretize",
    )(lam_re.reshape(G_C, 1, P_STATE), lam_im.reshape(G_C, 1, P_STATE), log_dt.reshape(G_C, 1, 1),
      jnp.swapaxes(b_re, 1, 2), jnp.swapaxes(b_im, 1, 2))


def _s5_kernel(u_ref, s0_ref, ar_ref, ai_ref, bd_ref, cd_ref, d_ref, gw_ref, gb_ref,
               yc_ref, slast_ref, bu_scr, st_scr, *, tt, n):
    @pl.when(pl.program_id(0) == 0)
    def _():
        st_scr[...] = s0_ref[...]

    u2 = u_ref[...].reshape(tt * n, D_C)
    ub = u2.astype(BF16)
    for hf in range(S5_HALVES):
        bu = jnp.dot(ub[:, hf * S5_HALF_IN:(hf + 1) * S5_HALF_IN], bd_ref[hf], preferred_element_type=F32)
        bu_scr[:, :, hf * S5_HALF_ST:(hf + 1) * S5_HALF_ST] = bu.reshape(tt, n, S5_HALF_ST)

    lane_blk = 4 * LANE_TILE
    for hf in range(S5_HALVES):
        for q in range(S5_HALF_RI // lane_blk):
            lr = hf * S5_HALF_ST + q * lane_blk
            li = lr + S5_HALF_RI
            la = hf * S5_HALF_RI + q * lane_blk
            ar = jnp.broadcast_to(ar_ref[:, la:la + lane_blk], (SUBLANE_TILE, lane_blk))
            ai = jnp.broadcast_to(ai_ref[:, la:la + lane_blk], (SUBLANE_TILE, lane_blk))

            def rows_body(rg, carry, lr=lr, li=li, ar=ar, ai=ai):
                rs = pl.ds(pl.multiple_of(rg * SUBLANE_TILE, SUBLANE_TILE), SUBLANE_TILE)

                def step(s, x):
                    xr, xi = x
                    nr = ar * xr - ai * xi + bu_scr[s, rs, lr:lr + lane_blk]
                    ni = ar * xi + ai * xr + bu_scr[s, rs, li:li + lane_blk]
                    bu_scr[s, rs, lr:lr + lane_blk] = nr
                    bu_scr[s, rs, li:li + lane_blk] = ni
                    return nr, ni

                x0 = (st_scr[rs, lr:lr + lane_blk], st_scr[rs, li:li + lane_blk])
                xr, xi = lax.fori_loop(0, tt, step, x0, unroll=min(tt, 8))
                st_scr[rs, lr:lr + lane_blk] = xr
                st_scr[rs, li:li + lane_blk] = xi
                return carry

            if n == SUBLANE_TILE:
                xr, xi = st_scr[:, lr:lr + lane_blk], st_scr[:, li:li + lane_blk]
                for s in range(tt):
                    nr = ar * xr - ai * xi + bu_scr[s, :, lr:lr + lane_blk]
                    ni = ar * xi + ai * xr + bu_scr[s, :, li:li + lane_blk]
                    bu_scr[s, :, lr:lr + lane_blk] = nr
                    bu_scr[s, :, li:li + lane_blk] = ni
                    xr, xi = nr, ni
                st_scr[:, lr:lr + lane_blk] = xr
                st_scr[:, li:li + lane_blk] = xi
            else:
                lax.fori_loop(0, n // SUBLANE_TILE, rows_body, 0)

    slast_ref[...] = st_scr[...]
    ys = []
    for hf in range(S5_HALVES):
        xs = bu_scr[:, :, hf * S5_HALF_ST:(hf + 1) * S5_HALF_ST].reshape(tt * n, S5_HALF_ST)
        ys.append(_bdot(xs, cd_ref[hf]))
    y = jnp.concatenate(ys, axis=-1) + d_ref[...] * u2
    z = jax.nn.gelu(y)
    yc = z * jax.nn.sigmoid(_bdot(z, gw_ref[...]) + gb_ref[...])
    yc_ref[...] = yc.astype(BF16)


def _mixer_c(proj, t, n, s0, ar, ai, bd, cd, d, glu_w, glu_b):
    tt = min(t, 512 // n) if n <= 512 else 1
    proj3 = proj.reshape(t, n, PA_COLS)
    yc, s_last = pl.pallas_call(
        functools.partial(_s5_kernel, tt=tt, n=n),
        grid=(t // tt,),
        in_specs=[
            pl.BlockSpec((tt, n, COL_BLK), lambda i: (i, 0, BLK_UC)),
            _resident((n, S5_LANES)),
            _resident((1, S5_LANES // 2)),
            _resident((1, S5_LANES // 2)),
            _resident((S5_HALVES, S5_HALF_IN, S5_HALF_ST)),
            _resident((S5_HALVES, S5_HALF_ST, S5_HALF_IN)),
            _resident((1, D_C)),
            _resident((D_C, D_C)),
            _resident((1, D_C)),
        ],
        out_specs=[
            pl.BlockSpec((tt * n, D_C), lambda i: (i, 0)),
            _resident((n, S5_LANES)),
        ],
        out_shape=[
            jax.ShapeDtypeStruct((t * n, D_C), BF16),
            jax.ShapeDtypeStruct((n, S5_LANES), F32),
        ],
        scratch_shapes=[
            pltpu.VMEM((tt, n, S5_LANES), F32),
            pltpu.VMEM((n, S5_LANES), F32),
        ],
        compiler_params=_params(("arbitrary",)),
        name="mixer_c",
    )(proj3, s0, ar, ai, bd, cd, d, glu_w, glu_b)
    return yc, s_last


def _single(shape):
    zeros = (0,) * len(shape)
    return pl.BlockSpec(shape, lambda *_: zeros, pipeline_mode=pl.Buffered(1))


def _merge_kernel(ya_ref, yb_ref, yc_ref, g_ref, x_ref, wbr_ref, wout_ref, o_ref):
    m = None
    for k, y_ref in enumerate((ya_ref, yb_ref, yc_ref)):
        p = _bdot(y_ref[...], wbr_ref[k])
        g = g_ref[:, k * D_MODEL:(k + 1) * D_MODEL].astype(F32)
        m = g * p if m is None else m + g * p
    o_ref[...] = x_ref[...] + _bdot(m, wout_ref[...])


def _merge(ya, yb, yc, gates, x, wbr, wout):
    r = x.shape[0]
    tm = min(r, 512)
    row = pl.BlockSpec((tm, D_MODEL), lambda i: (i, 0))
    return pl.pallas_call(
        _merge_kernel,
        grid=(r // tm,),
        in_specs=[pl.BlockSpec((tm, D_BR), lambda i: (i, 0))] * N_BRANCH + [
            pl.BlockSpec((tm, N_BRANCH * D_MODEL), lambda i: (i, 0)),
            row,
            _single((N_BRANCH, D_BR, D_MODEL)),
            _single((D_MODEL, D_MODEL)),
        ],
        out_specs=row,
        out_shape=jax.ShapeDtypeStruct((r, D_MODEL), F32),
        compiler_params=_params(("parallel",)),
        name="merge",
    )(ya, yb, yc, gates, x, wbr, wout)


def _swiglu(hb, wg_ref, wu_ref, wd_ref):
    hidden = wg_ref.shape[-1]
    out = None
    for c0 in range(0, hidden, FFN_CHUNK):
        cols = slice(c0, min(c0 + FFN_CHUNK, hidden))
        gate = jnp.dot(hb, wg_ref[:, cols], preferred_element_type=F32)
        up = jnp.dot(hb, wu_ref[:, cols], preferred_element_type=F32)
        part = _bdot(jax.nn.silu(gate) * up, wd_ref[cols, :])
        out = part if out is None else out + part
    return out


def _dense_ffn_kernel(x_ref, nf_ref, wg_ref, wu_ref, wd_ref, fin_ref, o_ref, *, final):
    x1 = x_ref[...]
    hb = _rms(x1, nf_ref[...]).astype(BF16)
    x2 = x1 + _swiglu(hb, wg_ref, wu_ref, wd_ref)
    o_ref[...] = _rms(x2, fin_ref[...]) if final else x2


def _dense_ffn(x1, nf, wg, wu, wd, fin, final):
    r = x1.shape[0]
    tm = min(r, 1024)
    row = pl.BlockSpec((tm, D_MODEL), lambda i: (i, 0))
    return pl.pallas_call(
        functools.partial(_dense_ffn_kernel, final=final),
        grid=(r // tm,),
        in_specs=[
            row,
            _resident((1, D_MODEL)),
            _single((D_MODEL, D_FF)),
            _single((D_MODEL, D_FF)),
            _single((D_FF, D_MODEL)),
            _resident((1, D_MODEL)),
        ],
        out_specs=row,
        out_shape=jax.ShapeDtypeStruct((r, D_MODEL), F32),
        compiler_params=_params(("parallel",)),
        name="dense_ffn",
    )(x1, nf, wg, wu, wd, fin)


def _moe_kernel(x_ref, nf_ref, rw_ref, rb_ref, wg_ref, wu_ref, wd_ref, fin_ref, o_ref,
                hb_scr, comb_scr, acc_scr, *, final):
    e = pl.program_id(1)
    lane = lax.broadcasted_iota(jnp.int32, comb_scr.shape, 1)

    @pl.when(e == 0)
    def _():
        hn = _rms(x_ref[...], nf_ref[...])
        hb_scr[...] = hn.astype(BF16)
        hn_hi = hn.astype(BF16)
        hn_lo = (hn - hn_hi.astype(F32)).astype(BF16)
        rw = rw_ref[...]
        rw_hi = rw.astype(BF16)
        rw_lo = (rw - rw_hi.astype(F32)).astype(BF16)
        logits = (jnp.dot(hn_hi, rw_hi, preferred_element_type=F32)
                  + (jnp.dot(hn_lo, rw_hi, preferred_element_type=F32)
                     + jnp.dot(hn_hi, rw_lo, preferred_element_type=F32))) + rb_ref[...]
        neg = jnp.float32(-jnp.inf)
        lg = jnp.where(lane < N_EXPERTS, logits, neg)
        m1 = jnp.max(lg, axis=-1, keepdims=True)
        i1 = jnp.min(jnp.where(lg == m1, lane, LANE_TILE), axis=-1, keepdims=True)
        lg2 = jnp.where(lane == i1, neg, lg)
        m2 = jnp.max(lg2, axis=-1, keepdims=True)
        i2 = jnp.min(jnp.where(lg2 == m2, lane, LANE_TILE), axis=-1, keepdims=True)
        e2 = jnp.exp(m2 - m1)
        den = 1.0 + e2
        comb_scr[...] = jnp.where(lane == i1, 1.0 / den, 0.0) + jnp.where(lane == i2, e2 / den, 0.0)
        acc_scr[...] = jnp.zeros_like(acc_scr)

    out_e = _swiglu(hb_scr[...], wg_ref, wu_ref, wd_ref)
    w_e = jnp.sum(jnp.where(lane == e, comb_scr[...], 0.0), axis=-1, keepdims=True)
    acc_scr[...] += w_e * out_e

    @pl.when(e == N_EXPERTS - 1)
    def _():
        x2 = x_ref[...] + acc_scr[...]
        o_ref[...] = _rms(x2, fin_ref[...]) if final else x2


def _moe_ffn(x1, nf, rw, rb, wg, wu, wd, fin, final):
    r = x1.shape[0]
    tm = min(r, 1024)
    row = pl.BlockSpec((tm, D_MODEL), lambda i, e: (i, 0))
    return pl.pallas_call(
        functools.partial(_moe_kernel, final=final),
        grid=(r // tm, N_EXPERTS),
        in_specs=[
            row,
            _resident((1, D_MODEL)),
            _single((D_MODEL, LANE_TILE)),
            _resident((1, LANE_TILE)),
            pl.BlockSpec((None, D_MODEL, D_EXP), lambda i, e: (e, 0, 0)),
            pl.BlockSpec((None, D_MODEL, D_EXP), lambda i, e: (e, 0, 0)),
            pl.BlockSpec((None, D_EXP, D_MODEL), lambda i, e: (e, 0, 0)),
            _resident((1, D_MODEL)),
        ],
        out_specs=row,
        out_shape=jax.ShapeDtypeStruct((r, D_MODEL), F32),
        scratch_shapes=[
            pltpu.VMEM((tm, D_MODEL), BF16),
            pltpu.VMEM((tm, LANE_TILE), F32),
            pltpu.VMEM((tm, D_MODEL), F32),
        ],
        compiler_params=_params(("parallel", "arbitrary")),
        name="moe_ffn",
    )(x1, nf, rw, rb, wg, wu, wd, fin)


def _block_diag(w):
    h, i, j = w.shape
    eye = jnp.eye(h, dtype=w.dtype)
    return jnp.einsum("hij,hk->hikj", w, eye).reshape(h * i, h * j)


def _s5_in_matrix(bbr, bbi):
    gh = G_C // S5_HALVES
    out = []
    for part in (bbr, bbi):
        p4 = part.reshape(S5_HALVES, gh, HG_C, P_STATE)
        eye = jnp.eye(gh, dtype=part.dtype)
        out.append(jnp.einsum("aghp,gk->aghkp", p4, eye).reshape(S5_HALVES, gh * HG_C, gh * P_STATE))
    return jnp.concatenate(out, axis=-1).astype(BF16)


def _s5_out_matrix(c_re, c_im):
    gh = G_C // S5_HALVES
    out = []
    for part in (c_re, -c_im):
        p4 = part.reshape(S5_HALVES, gh, HG_C, P_STATE)
        eye = jnp.eye(gh, dtype=part.dtype)
        out.append(jnp.einsum("agop,gk->agpko", p4, eye).reshape(S5_HALVES, gh * P_STATE, gh * HG_C))
    return jnp.concatenate(out, axis=1).astype(BF16)


def _s5_state_in(re, im):
    n = re.shape[0]
    parts = [a.reshape(n, S5_HALVES, 1, S5_HALF_RI) for a in (re, im)]
    return jnp.concatenate(parts, axis=2).reshape(n, S5_LANES)


def _s5_state_out(s):
    n = s.shape[0]
    s4 = s.reshape(n, S5_HALVES, 2, S5_HALF_RI)
    return s4[:, :, 0].reshape(n, G_C, P_STATE), s4[:, :, 1].reshape(n, G_C, P_STATE)


def _lane_vec(v):
    return v.reshape(1, -1)


def _layer_weights(l, w):
    ar, ai, bbr, bbi = _s5_discretize(w["c_lam_re"][l], w["c_lam_im"][l], w["c_log_dt"][l],
                                      w["c_b_re"][l], w["c_b_im"][l])
    causal = jnp.tril(jnp.ones((CHUNK, CHUNK), dtype=bool))
    ws = w["a_ws"][l]
    lw = dict(
        norm_mix_g=_lane_vec(w["norm_mix_g"][l]),
        w_in=w["w_in"][l].astype(BF16),
        b_gate=_lane_vec(w["b_gate"][l]),
        a_ln_g=_lane_vec(w["a_ln_g"][l]),
        a_ln_b=_lane_vec(w["a_ln_b"][l]),
        a_wm=jnp.where(causal[None], ws, jnp.zeros_like(ws)),
        a_ws=ws,
        a_bs=w["a_bs"][l],
        b_cw=w["b_conv_w"][l][:, None, :],
        b_cb=_lane_vec(w["b_conv_b"][l]),
        b_wax=jnp.concatenate([_block_diag(w["b_wa"][l]), _block_diag(w["b_wx"][l])], axis=1).astype(BF16),
        b_ba=_lane_vec(w["b_ba"][l]),
        b_bx=_lane_vec(w["b_bx"][l]),
        b_lam=_lane_vec(w["b_lam"][l]),
        c_ar=ar.reshape(S5_HALVES, S5_HALF_RI).reshape(1, S5_LANES // 2),
        c_ai=ai.reshape(S5_HALVES, S5_HALF_RI).reshape(1, S5_LANES // 2),
        c_bd=_s5_in_matrix(bbr, bbi),
        c_cd=_s5_out_matrix(w["c_c_re"][l], w["c_c_im"][l]),
        c_d=_lane_vec(w["c_d"][l]),
        c_glu_w=w["c_glu_w"][l].astype(BF16),
        c_glu_b=_lane_vec(w["c_glu_b"][l]),
        w_branch=w["w_branch"][l].astype(BF16),
        w_out=w["w_out"][l].astype(BF16),
        norm_ffn_g=_lane_vec(w["norm_ffn_g"][l]),
        final_norm_g=_lane_vec(w["final_norm_g"]),
    )
    j = l // 2
    if l % 2 == 0:
        lw.update(ffn_wg=w["ffn_w_gate"][j].astype(BF16), ffn_wu=w["ffn_w_up"][j].astype(BF16),
                  ffn_wd=w["ffn_w_down"][j].astype(BF16))
    else:
        pad = LANE_TILE - N_EXPERTS
        lw.update(moe_rw=jnp.pad(w["moe_router_w"][j], ((0, 0), (0, pad))),
                  moe_rb=jnp.pad(w["moe_router_b"][j], (0, pad)).reshape(1, LANE_TILE),
                  moe_wg=w["moe_w_gate"][j].astype(BF16), moe_wu=w["moe_w_up"][j].astype(BF16),
                  moe_wd=w["moe_w_down"][j].astype(BF16))
    return lw


def _trunk(x_tm, t, n, conv_s, lru_s, s5re_s, s5im_s, layers):
    x = x_tm
    vas, convs, lrus, sres, sims = [], [], [], [], []
    for l, lw in enumerate(layers):
        proj, gates = _in_proj(x, lw["norm_mix_g"], lw["w_in"], lw["b_gate"], lw["a_ln_g"], lw["a_ln_b"])
        if t % CHUNK == 0:
            bk = jnp.broadcast_to(lw["a_bs"].T[:, None, :, None], (CHUNK, n, H_A, HD_A))
            ya = _mixer_a_prompt(proj, t, n, lw["a_wm"].astype(BF16), bk.reshape(CHUNK * n, D_A))
        else:
            assert t < CHUNK
            wv = jnp.repeat(jnp.transpose(lw["a_ws"][:, :t, :t], (1, 2, 0)), HD_A, axis=-1)[:, :, None, :]
            bv = jnp.repeat(jnp.transpose(lw["a_bs"][:, :t], (1, 0)), HD_A, axis=-1)[:, None, :]
            ya = _mixer_a_sample(proj, t, n, wv, bv)
        yb, conv_new, h_last = _mixer_b(proj, t, n, jnp.swapaxes(conv_s[l], 0, 1), lru_s[l], lw["b_cw"],
                                        lw["b_cb"], lw["b_wax"], lw["b_ba"], lw["b_bx"], lw["b_lam"])
        yc, s_last = _mixer_c(proj, t, n, _s5_state_in(s5re_s[l], s5im_s[l]), lw["c_ar"], lw["c_ai"],
                              lw["c_bd"], lw["c_cd"], lw["c_d"], lw["c_glu_w"], lw["c_glu_b"])
        final = l == len(layers) - 1
        x1 = _merge(ya, yb, yc, gates, x, lw["w_branch"], lw["w_out"])
        if l % 2 == 0:
            x = _dense_ffn(x1, lw["norm_ffn_g"], lw["ffn_wg"], lw["ffn_wu"], lw["ffn_wd"],
                           lw["final_norm_g"], final)
        else:
            x = _moe_ffn(x1, lw["norm_ffn_g"], lw["moe_rw"], lw["moe_rb"], lw["moe_wg"], lw["moe_wu"],
                         lw["moe_wd"], lw["final_norm_g"], final)
        va = proj.reshape(t, n, PA_COLS)[:, :, BLK_VA * COL_BLK:(BLK_VA + 1) * COL_BLK]
        vas.append(jnp.swapaxes(va, 0, 1))
        convs.append(jnp.swapaxes(conv_new, 0, 1))
        lrus.append(h_last)
        s_re, s_im = _s5_state_out(s_last)
        sres.append(s_re)
        sims.append(s_im)
    return x, jnp.stack(vas), jnp.stack(convs), jnp.stack(lrus), jnp.stack(sres), jnp.stack(sims)


def kernel(x_prompt, x_sample, state_conv_b, state_lru_b, state_s5_re, state_s5_im, norm_mix_g, w_in, b_gate, a_ln_g, a_ln_b, a_ws, a_bs, b_conv_w, b_conv_b, b_wa, b_ba, b_wx, b_bx, b_lam, c_lam_re, c_lam_im, c_log_dt, c_b_re, c_b_im, c_c_re, c_c_im, c_d, c_glu_w, c_glu_b, w_branch, w_out, norm_ffn_g, ffn_w_gate, ffn_w_up, ffn_w_down, moe_router_w, moe_router_b, moe_w_gate, moe_w_up, moe_w_down, final_norm_g):
    w = dict(norm_mix_g=norm_mix_g, w_in=w_in, b_gate=b_gate, a_ln_g=a_ln_g, a_ln_b=a_ln_b, a_ws=a_ws,
             a_bs=a_bs, b_conv_w=b_conv_w, b_conv_b=b_conv_b, b_wa=b_wa, b_ba=b_ba, b_wx=b_wx, b_bx=b_bx,
             b_lam=b_lam, c_lam_re=c_lam_re, c_lam_im=c_lam_im, c_log_dt=c_log_dt, c_b_re=c_b_re,
             c_b_im=c_b_im, c_c_re=c_c_re, c_c_im=c_c_im, c_d=c_d, c_glu_w=c_glu_w, c_glu_b=c_glu_b,
             w_branch=w_branch, w_out=w_out, norm_ffn_g=norm_ffn_g, ffn_w_gate=ffn_w_gate,
             ffn_w_up=ffn_w_up, ffn_w_down=ffn_w_down, moe_router_w=moe_router_w,
             moe_router_b=moe_router_b, moe_w_gate=moe_w_gate, moe_w_up=moe_w_up, moe_w_down=moe_w_down,
             final_norm_g=final_norm_g)
    layers = [_layer_weights(l, w) for l in range(DEPTH)]

    nb, tp, _ = x_prompt.shape
    ns, ts, _ = x_sample.shape
    conv0 = jnp.zeros((DEPTH, nb, CONV_W - 1, D_B), F32)
    lru0 = jnp.zeros((DEPTH, nb, D_B), F32)
    s50 = jnp.zeros((DEPTH, nb, G_C, P_STATE), F32)
    xp = jnp.swapaxes(x_prompt, 0, 1).reshape(tp * nb, D_MODEL)
    yp, _, conv_p, lru_p, sre_p, sim_p = _trunk(xp, tp, nb, conv0, lru0, s50, s50, layers)
    xs = jnp.swapaxes(x_sample, 0, 1).reshape(ts * ns, D_MODEL)
    ys, va_s, conv_s, lru_s, sre_s, sim_s = _trunk(xs, ts, ns, state_conv_b, state_lru_b, state_s5_re,
                                                   state_s5_im, layers)
    y_prompt = jnp.swapaxes(yp.reshape(tp, nb, D_MODEL), 0, 1)
    y_sample = jnp.swapaxes(ys.reshape(ts, ns, D_MODEL), 0, 1)
    return (y_prompt, y_sample, conv_p, lru_p, sre_p, sim_p, va_s, conv_s, lru_s, sre_s, sim_s)
```

```python
import functools
import math

import jax
import jax.numpy as jnp
from jax import lax
from jax.experimental import pallas as pl
from jax.experimental.pallas import tpu as pltpu

D_MODEL = 1024
DEPTH = 2
CHUNK = 128
D_A = D_MODEL // 2
H_A = 4
HD_A = D_A // H_A
D_B = D_MODEL // 2
H_B = 8
BH_B = D_B // H_B
CONV_W = 4
LRU_C = 8.0
D_C = D_MODEL // 2
HG_C = 16
G_C = D_C // HG_C
P_STATE = 64
N_BRANCH = 3
D_BR = D_MODEL // 2
IN_COLS = 2 * D_A + 2 * D_B + D_C + N_BRANCH * D_MODEL
D_FF = 11 * D_MODEL // 4
N_EXPERTS = 8
TOP_K = 2
D_EXP = D_FF // 2
EPS = 1e-6

F32 = jnp.float32
BF16 = jnp.bfloat16

COL_BLK = 512
N_COL_BLK = IN_COLS // COL_BLK
BLK_UA, BLK_VA, BLK_XB, BLK_GB, BLK_UC, BLK_GATE = 0, 1, 2, 3, 4, 5
PA_COLS = BLK_GATE * COL_BLK
S5_HALVES = 2
S5_HALF_IN = D_C // S5_HALVES
S5_HALF_RI = (G_C // S5_HALVES) * P_STATE
S5_HALF_ST = 2 * S5_HALF_RI
S5_LANES = S5_HALVES * S5_HALF_ST
LANE_TILE = 128
SUBLANE_TILE = 8
MXU_DIM = 256
FFN_CHUNK = 3 * MXU_DIM
VMEM_LIMIT = 56 * 1024 * 1024


def _resident(shape):
    zeros = (0,) * len(shape)
    return pl.BlockSpec(shape, lambda *_: zeros)


def _params(sem):
    return pltpu.CompilerParams(dimension_semantics=sem, vmem_limit_bytes=VMEM_LIMIT)


def _rms(x, g):
    return x * lax.rsqrt(jnp.mean(x * x, axis=-1, keepdims=True) + EPS) * g


def _bdot(a, b):
    return jnp.dot(a.astype(BF16), b, preferred_element_type=F32)


def _in_proj_kernel(x_ref, g_ref, w_ref, bg_ref, lng_ref, lnb_ref, pa_ref, gate_ref):
    half = x_ref.shape[0] // 2
    for r0 in (0, half):
        rows = slice(r0, r0 + half)
        hb = _rms(x_ref[rows, :], g_ref[...]).astype(BF16)
        for j in range(N_COL_BLK):
            cols = slice(j * COL_BLK, (j + 1) * COL_BLK)
            acc = jnp.dot(hb, w_ref[:, cols], preferred_element_type=F32)
            if j in (BLK_UA, BLK_GB):
                pa_ref[rows, cols] = jax.nn.gelu(acc)
            elif j == BLK_VA:
                v = jax.nn.gelu(acc)
                vc = v - jnp.mean(v, axis=-1, keepdims=True)
                var = jnp.mean(vc * vc, axis=-1, keepdims=True)
                pa_ref[rows, cols] = vc * lax.rsqrt(var + EPS) * lng_ref[...] + lnb_ref[...]
            elif j in (BLK_XB, BLK_UC):
                pa_ref[rows, cols] = acc
            else:
                gcols = slice(cols.start - PA_COLS, cols.stop - PA_COLS)
                gate_ref[rows, gcols] = jax.nn.sigmoid(acc + bg_ref[:, gcols]).astype(BF16)


def _in_proj(x, norm_g, w_in, b_gate, ln_g, ln_b):
    r = x.shape[0]
    tm = min(r, 512)
    return pl.pallas_call(
        _in_proj_kernel,
        grid=(r // tm,),
        in_specs=[
            pl.BlockSpec((tm, D_MODEL), lambda i: (i, 0)),
            _resident((1, D_MODEL)),
            _single((D_MODEL, IN_COLS)),
            _resident((1, N_BRANCH * D_MODEL)),
            _resident((1, D_A)),
            _resident((1, D_A)),
        ],
        out_specs=[
            pl.BlockSpec((tm, PA_COLS), lambda i: (i, 0)),
            pl.BlockSpec((tm, N_BRANCH * D_MODEL), lambda i: (i, 0)),
        ],
        out_shape=[
            jax.ShapeDtypeStruct((r, PA_COLS), F32),
            jax.ShapeDtypeStruct((r, N_BRANCH * D_MODEL), BF16),
        ],
        compiler_params=_params(("parallel",)),
        name="in_proj",
    )(x, norm_g, w_in, b_gate, ln_g, ln_b)


def _mixer_a_mxu_kernel(u_ref, v_ref, wm_ref, bk_ref, o_ref, wk_scr, *, rows, n):
    @pl.when(pl.program_id(0) == 0)
    def _():
        shift = n.bit_length() - 1
        e_rows = (lax.broadcasted_iota(jnp.int32, (rows, CHUNK), 0) >> shift
                  == lax.broadcasted_iota(jnp.int32, (rows, CHUNK), 1)).astype(BF16)
        e_cols = (lax.broadcasted_iota(jnp.int32, (CHUNK, rows), 0)
                  == lax.broadcasted_iota(jnp.int32, (CHUNK, rows), 1) >> shift).astype(BF16)
        same_seq = ((lax.broadcasted_iota(jnp.int32, (rows, rows), 0) & (n - 1))
                    == (lax.broadcasted_iota(jnp.int32, (rows, rows), 1) & (n - 1)))
        for h in range(H_A):
            left = jnp.dot(e_rows, wm_ref[h], preferred_element_type=F32).astype(BF16)
            full = jnp.dot(left, e_cols, preferred_element_type=F32)
            wk_scr[h] = jnp.where(same_seq, full, 0.0).astype(BF16)

    vb = v_ref[...].astype(BF16)
    for h in range(H_A):
        cols = slice(h * HD_A, (h + 1) * HD_A)
        for r0 in range(0, rows, MXU_DIM):
            k = r0 + MXU_DIM
            mixed = jnp.dot(wk_scr[h, r0:k, 0:k], vb[0:k, cols], preferred_element_type=F32)
            o_ref[r0:k, cols] = (u_ref[r0:k, cols] * (mixed + bk_ref[r0:k, cols])).astype(BF16)


def _mixer_a_prompt(pa, t, n, wm, bk):
    rows = CHUNK * n
    assert rows % MXU_DIM == 0 and MXU_DIM % n == 0 and n & (n - 1) == 0
    return pl.pallas_call(
        functools.partial(_mixer_a_mxu_kernel, rows=rows, n=n),
        grid=(t // CHUNK,),
        in_specs=[
            pl.BlockSpec((rows, COL_BLK), lambda c: (c, BLK_UA)),
            pl.BlockSpec((rows, COL_BLK), lambda c: (c, BLK_VA)),
            _single((H_A, CHUNK, CHUNK)),
            _single((rows, D_A)),
        ],
        out_specs=pl.BlockSpec((rows, D_A), lambda c: (c, 0)),
        out_shape=jax.ShapeDtypeStruct((t * n, D_A), BF16),
        scratch_shapes=[pltpu.VMEM((H_A, rows, rows), BF16)],
        compiler_params=_params(("arbitrary",)),
        name="mixer_a_prompt",
    )(pa, pa, wm, bk)


def _mixer_a_vpu_kernel(u_ref, v_ref, wv_ref, bv_ref, o_ref, *, t, n):
    for i in range(t):
        mixed = wv_ref[i, 0] * v_ref[0]
        for s in range(1, i + 1):
            mixed = mixed + wv_ref[i, s] * v_ref[s]
        o_ref[i * n:(i + 1) * n] = (u_ref[i] * (mixed + bv_ref[i])).astype(BF16)


def _mixer_a_sample(proj, t, n, wv, bv):
    proj3 = proj.reshape(t, n, PA_COLS)
    return pl.pallas_call(
        functools.partial(_mixer_a_vpu_kernel, t=t, n=n),
        grid=(1,),
        in_specs=[
            pl.BlockSpec((t, n, COL_BLK), lambda i: (0, 0, BLK_UA)),
            pl.BlockSpec((t, n, COL_BLK), lambda i: (0, 0, BLK_VA)),
            _resident((t, t, 1, D_A)),
            _resident((t, 1, D_A)),
        ],
        out_specs=_resident((t * n, D_A)),
        out_shape=jax.ShapeDtypeStruct((t * n, D_A), BF16),
        compiler_params=_params(("arbitrary",)),
        name="mixer_a_sample",
    )(proj3, proj3, wv, bv)


def _lru_kernel(xb_ref, gb_ref, conv0_ref, h0_ref, cw_ref, cb_ref, wax_ref, ba_ref, bx_ref, lam_ref,
                yb_ref, convn_ref, hlast_ref, xp_scr, a_scr, d_scr, h_scr, *, tt, n):
    @pl.when(pl.program_id(0) == 0)
    def _():
        xp_scr[0:CONV_W - 1] = conv0_ref[...]
        h_scr[...] = h0_ref[...]

    xp_scr[CONV_W - 1:CONV_W - 1 + tt] = xb_ref[...]
    xc = cb_ref[...] + xp_scr[0:tt] * cw_ref[0]
    for k in range(1, CONV_W):
        xc = xc + xp_scr[k:k + tt] * cw_ref[k]
    tail = xp_scr[tt:tt + CONV_W - 1]
    convn_ref[...] = tail
    xp_scr[0:CONV_W - 1] = tail

    xc2 = xc.reshape(tt * n, D_B)
    pre = _bdot(xc2, wax_ref[...])
    r = jax.nn.sigmoid(pre[:, :D_B] + ba_ref[...])
    i = jax.nn.sigmoid(pre[:, D_B:] + bx_ref[...])
    log_a = -LRU_C * r * jax.nn.softplus(-lam_ref[...])
    a = jnp.exp(log_a)
    a_scr[...] = a.reshape(tt, n, D_B)
    gain = jnp.sqrt(-jnp.tanh(log_a) * (a * a + 1.0))
    d_scr[...] = (gain * (i * xc2)).reshape(tt, n, D_B)

    def step(s, h):
        h = a_scr[s] * h + d_scr[s]
        d_scr[s] = h
        return h

    h = h_scr[...]
    if n == SUBLANE_TILE:
        for s in range(tt):
            h = step(s, h)
    else:
        h = lax.fori_loop(0, tt, step, h, unroll=min(tt, 8))
    h_scr[...] = h
    hlast_ref[...] = h
    yb_ref[...] = (d_scr[...] * gb_ref[...]).reshape(tt * n, D_B).astype(BF16)


def _mixer_b(proj, t, n, conv0, h0, cw, cb, wax, ba, bx, lam):
    tt = min(t, 512 // n) if n <= 512 else 1
    proj3 = proj.reshape(t, n, PA_COLS)
    yb, conv_new, h_last = pl.pallas_call(
        functools.partial(_lru_kernel, tt=tt, n=n),
        grid=(t // tt,),
        in_specs=[
            pl.BlockSpec((tt, n, COL_BLK), lambda i: (i, 0, BLK_XB)),
            pl.BlockSpec((tt, n, COL_BLK), lambda i: (i, 0, BLK_GB)),
            _resident((CONV_W - 1, n, D_B)),
            _resident((n, D_B)),
            _resident((CONV_W, 1, D_B)),
            _resident((1, D_B)),
            _resident((D_B, 2 * D_B)),
            _resident((1, D_B)),
            _resident((1, D_B)),
            _resident((1, D_B)),
        ],
        out_specs=[
            pl.BlockSpec((tt * n, D_B), lambda i: (i, 0)),
            _resident((CONV_W - 1, n, D_B)),
            _resident((n, D_B)),
        ],
        out_shape=[
            jax.ShapeDtypeStruct((t * n, D_B), BF16),
            jax.ShapeDtypeStruct((CONV_W - 1, n, D_B), F32),
            jax.ShapeDtypeStruct((n, D_B), F32),
        ],
        scratch_shapes=[
            pltpu.VMEM((tt + CONV_W - 1, n, D_B), F32),
            pltpu.VMEM((tt, n, D_B), F32),
            pltpu.VMEM((tt, n, D_B), F32),
            pltpu.VMEM((n, D_B), F32),
        ],
        compiler_params=_params(("arbitrary",)),
        name="mixer_b",
    )(proj3, proj3, conv0, h0, cw, cb, wax, ba, bx, lam)
    return yb, conv_new, h_last


def _s5_disc_kernel(lr_ref, li_ref, ldt_ref, bre_ref, bim_ref, ar_ref, ai_ref, bbr_ref, bbi_ref):
    lr = lr_ref[...]
    li = li_ref[...]
    dt = jnp.exp(ldt_ref[...])
    mag = jnp.exp(lr * dt)
    ar = mag * jnp.cos(li * dt)
    ai = mag * jnp.sin(li * dt)
    ar_ref[...] = ar
    ai_ref[...] = ai
    den = lr * lr + li * li
    qr = ((ar - 1.0) * lr + ai * li) / den
    qi = (ai * lr - (ar - 1.0) * li) / den
    bre = bre_ref[...]
    bim = bim_ref[...]
    bbr_ref[...] = qr * bre - qi * bim
    bbi_ref[...] = qr * bim + qi * bre


def _s5_discretize(lam_re, lam_im, log_dt, b_re, b_im):
    g3 = jax.ShapeDtypeStruct((G_C, 1, P_STATE), F32)
    b3 = jax.ShapeDtypeStruct((G_C, HG_C, P_STATE), F32)
    return pl.pallas_call(
        _s5_disc_kernel,
        out_shape=[g3, g3, b3, b3],
        name="s5_discretize",
    )(lam_re.reshape(G_C, 1, P_STATE), lam_im.reshape(G_C, 1, P_STATE), log_dt.reshape(G_C, 1, 1),
      jnp.swapaxes(b_re, 1, 2), jnp.swapaxes(b_im, 1, 2))


def _s5_kernel(u_ref, s0_ref, ar_ref, ai_ref, bd_ref, cd_ref, d_ref, gw_ref, gb_ref,
               yc_ref, slast_ref, bu_scr, st_scr, *, tt, n):
    @pl.when(pl.program_id(0) == 0)
    def _():
        st_scr[...] = s0_ref[...]

    u2 = u_ref[...].reshape(tt * n, D_C)
    ub = u2.astype(BF16)
    for hf in range(S5_HALVES):
        bu = jnp.dot(ub[:, hf * S5_HALF_IN:(hf + 1) * S5_HALF_IN], bd_ref[hf], preferred_element_type=F32)
        bu_scr[:, :, hf * S5_HALF_ST:(hf + 1) * S5_HALF_ST] = bu.reshape(tt, n, S5_HALF_ST)

    lane_blk = 4 * LANE_TILE
    for hf in range(S5_HALVES):
        for q in range(S5_HALF_RI // lane_blk):
            lr = hf * S5_HALF_ST + q * lane_blk
            li = lr + S5_HALF_RI
            la = hf * S5_HALF_RI + q * lane_blk
            ar = jnp.broadcast_to(ar_ref[:, la:la + lane_blk], (SUBLANE_TILE, lane_blk))
            ai = jnp.broadcast_to(ai_ref[:, la:la + lane_blk], (SUBLANE_TILE, lane_blk))

            def rows_body(rg, carry, lr=lr, li=li, ar=ar, ai=ai):
                rs = pl.ds(pl.multiple_of(rg * SUBLANE_TILE, SUBLANE_TILE), SUBLANE_TILE)

                def step(s, x):
                    xr, xi = x
                    nr = ar * xr - ai * xi + bu_scr[s, rs, lr:lr + lane_blk]
                    ni = ar * xi + ai * xr + bu_scr[s, rs, li:li + lane_blk]
                    bu_scr[s, rs, lr:lr + lane_blk] = nr
                    bu_scr[s, rs, li:li + lane_blk] = ni
                    return nr, ni

                x0 = (st_scr[rs, lr:lr + lane_blk], st_scr[rs, li:li + lane_blk])
                xr, xi = lax.fori_loop(0, tt, step, x0, unroll=min(tt, 8))
                st_scr[rs, lr:lr + lane_blk] = xr
                st_scr[rs, li:li + lane_blk] = xi
                return carry

            if n == SUBLANE_TILE:
                xr, xi = st_scr[:, lr:lr + lane_blk], st_scr[:, li:li + lane_blk]
                for s in range(tt):
                    nr = ar * xr - ai * xi + bu_scr[s, :, lr:lr + lane_blk]
                    ni = ar * xi + ai * xr + bu_scr[s, :, li:li + lane_blk]
                    bu_scr[s, :, lr:lr + lane_blk] = nr
                    bu_scr[s, :, li:li + lane_blk] = ni
                    xr, xi = nr, ni
                st_scr[:, lr:lr + lane_blk] = xr
                st_scr[:, li:li + lane_blk] = xi
            else:
                lax.fori_loop(0, n // SUBLANE_TILE, rows_body, 0)

    slast_ref[...] = st_scr[...]
    ys = []
    for hf in range(S5_HALVES):
        xs = bu_scr[:, :, hf * S5_HALF_ST:(hf + 1) * S5_HALF_ST].reshape(tt * n, S5_HALF_ST)
        ys.append(_bdot(xs, cd_ref[hf]))
    y = jnp.concatenate(ys, axis=-1) + d_ref[...] * u2
    z = jax.nn.gelu(y)
    yc = z * jax.nn.sigmoid(_bdot(z, gw_ref[...]) + gb_ref[...])
    yc_ref[...] = yc.astype(BF16)


def _mixer_c(proj, t, n, s0, ar, ai, bd, cd, d, glu_w, glu_b):
    tt = min(t, 512 // n) if n <= 512 else 1
    proj3 = proj.reshape(t, n, PA_COLS)
    yc, s_last = pl.pallas_call(
        functools.partial(_s5_kernel, tt=tt, n=n),
        grid=(t // tt,),
        in_specs=[
            pl.BlockSpec((tt, n, COL_BLK), lambda i: (i, 0, BLK_UC)),
            _resident((n, S5_LANES)),
            _resident((1, S5_LANES // 2)),
            _resident((1, S5_LANES // 2)),
            _resident((S5_HALVES, S5_HALF_IN, S5_HALF_ST)),
            _resident((S5_HALVES, S5_HALF_ST, S5_HALF_IN)),
            _resident((1, D_C)),
            _resident((D_C, D_C)),
            _resident((1, D_C)),
        ],
        out_specs=[
            pl.BlockSpec((tt * n, D_C), lambda i: (i, 0)),
            _resident((n, S5_LANES)),
        ],
        out_shape=[
            jax.ShapeDtypeStruct((t * n, D_C), BF16),
            jax.ShapeDtypeStruct((n, S5_LANES), F32),
        ],
        scratch_shapes=[
            pltpu.VMEM((tt, n, S5_LANES), F32),
            pltpu.VMEM((n, S5_LANES), F32),
        ],
        compiler_params=_params(("arbitrary",)),
        name="mixer_c",
    )(proj3, s0, ar, ai, bd, cd, d, glu_w, glu_b)
    return yc, s_last


def _single(shape):
    zeros = (0,) * len(shape)
    return pl.BlockSpec(shape, lambda *_: zeros, pipeline_mode=pl.Buffered(1))


def _merge_kernel(ya_ref, yb_ref, yc_ref, g_ref, x_ref, wbr_ref, wout_ref, o_ref):
    m = None
    for k, y_ref in enumerate((ya_ref, yb_ref, yc_ref)):
        p = _bdot(y_ref[...], wbr_ref[k])
        g = g_ref[:, k * D_MODEL:(k + 1) * D_MODEL].astype(F32)
        m = g * p if m is None else m + g * p
    o_ref[...] = x_ref[...] + _bdot(m, wout_ref[...])


def _merge(ya, yb, yc, gates, x, wbr, wout):
    r = x.shape[0]
    tm = min(r, 512)
    row = pl.BlockSpec((tm, D_MODEL), lambda i: (i, 0))
    return pl.pallas_call(
        _merge_kernel,
        grid=(r // tm,),
        in_specs=[pl.BlockSpec((tm, D_BR), lambda i: (i, 0))] * N_BRANCH + [
            pl.BlockSpec((tm, N_BRANCH * D_MODEL), lambda i: (i, 0)),
            row,
            _single((N_BRANCH, D_BR, D_MODEL)),
            _single((D_MODEL, D_MODEL)),
        ],
        out_specs=row,
        out_shape=jax.ShapeDtypeStruct((r, D_MODEL), F32),
        compiler_params=_params(("parallel",)),
        name="merge",
    )(ya, yb, yc, gates, x, wbr, wout)


def _swiglu(hb, wg_ref, wu_ref, wd_ref):
    hidden = wg_ref.shape[-1]
    out = None
    for c0 in range(0, hidden, FFN_CHUNK):
        cols = slice(c0, min(c0 + FFN_CHUNK, hidden))
        gate = jnp.dot(hb, wg_ref[:, cols], preferred_element_type=F32)
        up = jnp.dot(hb, wu_ref[:, cols], preferred_element_type=F32)
        part = _bdot(jax.nn.silu(gate) * up, wd_ref[cols, :])
        out = part if out is None else out + part
    return out


def _dense_ffn_kernel(x_ref, nf_ref, wg_ref, wu_ref, wd_ref, fin_ref, o_ref, *, final):
    x1 = x_ref[...]
    hb = _rms(x1, nf_ref[...]).astype(BF16)
    x2 = x1 + _swiglu(hb, wg_ref, wu_ref, wd_ref)
    o_ref[...] = _rms(x2, fin_ref[...]) if final else x2


def _dense_ffn(x1, nf, wg, wu, wd, fin, final):
    r = x1.shape[0]
    tm = min(r, 1024)
    row = pl.BlockSpec((tm, D_MODEL), lambda i: (i, 0))
    return pl.pallas_call(
        functools.partial(_dense_ffn_kernel, final=final),
        grid=(r // tm,),
        in_specs=[
            row,
            _resident((1, D_MODEL)),
            _single((D_MODEL, D_FF)),
            _single((D_MODEL, D_FF)),
            _single((D_FF, D_MODEL)),
            _resident((1, D_MODEL)),
        ],
        out_specs=row,
        out_shape=jax.ShapeDtypeStruct((r, D_MODEL), F32),
        compiler_params=_params(("parallel",)),
        name="dense_ffn",
    )(x1, nf, wg, wu, wd, fin)


def _moe_kernel(x_ref, nf_ref, rw_ref, rb_ref, wg_ref, wu_ref, wd_ref, fin_ref, o_ref,
                hb_scr, comb_scr, acc_scr, *, final):
    e = pl.program_id(1)
    lane = lax.broadcasted_iota(jnp.int32, comb_scr.shape, 1)

    @pl.when(e == 0)
    def _():
        hn = _rms(x_ref[...], nf_ref[...])
        hb_scr[...] = hn.astype(BF16)
        hn_hi = hn.astype(BF16)
        hn_lo = (hn - hn_hi.astype(F32)).astype(BF16)
        rw = rw_ref[...]
        rw_hi = rw.astype(BF16)
        rw_lo = (rw - rw_hi.astype(F32)).astype(BF16)
        logits = (jnp.dot(hn_hi, rw_hi, preferred_element_type=F32)
                  + (jnp.dot(hn_lo, rw_hi, preferred_element_type=F32)
                     + jnp.dot(hn_hi, rw_lo, preferred_element_type=F32))) + rb_ref[...]
        neg = jnp.float32(-jnp.inf)
        lg = jnp.where(lane < N_EXPERTS, logits, neg)
        m1 = jnp.max(lg, axis=-1, keepdims=True)
        i1 = jnp.min(jnp.where(lg == m1, lane, LANE_TILE), axis=-1, keepdims=True)
        lg2 = jnp.where(lane == i1, neg, lg)
        m2 = jnp.max(lg2, axis=-1, keepdims=True)
        i2 = jnp.min(jnp.where(lg2 == m2, lane, LANE_TILE), axis=-1, keepdims=True)
        e2 = jnp.exp(m2 - m1)
        den = 1.0 + e2
        comb_scr[...] = jnp.where(lane == i1, 1.0 / den, 0.0) + jnp.where(lane == i2, e2 / den, 0.0)
        acc_scr[...] = jnp.zeros_like(acc_scr)

    out_e = _swiglu(hb_scr[...], wg_ref, wu_ref, wd_ref)
    w_e = jnp.sum(jnp.where(lane == e, comb_scr[...], 0.0), axis=-1, keepdims=True)
    acc_scr[...] += w_e * out_e

    @pl.when(e == N_EXPERTS - 1)
    def _():
        x2 = x_ref[...] + acc_scr[...]
        o_ref[...] = _rms(x2, fin_ref[...]) if final else x2


def _moe_ffn(x1, nf, rw, rb, wg, wu, wd, fin, final):
    r = x1.shape[0]
    tm = min(r, 1024)
    row = pl.BlockSpec((tm, D_MODEL), lambda i, e: (i, 0))
    return pl.pallas_call(
        functools.partial(_moe_kernel, final=final),
        grid=(r // tm, N_EXPERTS),
        in_specs=[
            row,
            _resident((1, D_MODEL)),
            _single((D_MODEL, LANE_TILE)),
            _resident((1, LANE_TILE)),
            pl.BlockSpec((None, D_MODEL, D_EXP), lambda i, e: (e, 0, 0)),
            pl.BlockSpec((None, D_MODEL, D_EXP), lambda i, e: (e, 0, 0)),
            pl.BlockSpec((None, D_EXP, D_MODEL), lambda i, e: (e, 0, 0)),
            _resident((1, D_MODEL)),
        ],
        out_specs=row,
        out_shape=jax.ShapeDtypeStruct((r, D_MODEL), F32),
        scratch_shapes=[
            pltpu.VMEM((tm, D_MODEL), BF16),
            pltpu.VMEM((tm, LANE_TILE), F32),
            pltpu.VMEM((tm, D_MODEL), F32),
        ],
        compiler_params=_params(("parallel", "arbitrary")),
        name="moe_ffn",
    )(x1, nf, rw, rb, wg, wu, wd, fin)


def _block_diag(w):
    h, i, j = w.shape
    eye = jnp.eye(h, dtype=w.dtype)
    return jnp.einsum("hij,hk->hikj", w, eye).reshape(h * i, h * j)


def _s5_in_matrix(bbr, bbi):
    gh = G_C // S5_HALVES
    out = []
    for part in (bbr, bbi):
        p4 = part.reshape(S5_HALVES, gh, HG_C, P_STATE)
        eye = jnp.eye(gh, dtype=part.dtype)
        out.append(jnp.einsum("aghp,gk->aghkp", p4, eye).reshape(S5_HALVES, gh * HG_C, gh * P_STATE))
    return jnp.concatenate(out, axis=-1).astype(BF16)


def _s5_out_matrix(c_re, c_im):
    gh = G_C // S5_HALVES
    out = []
    for part in (c_re, -c_im):
        p4 = part.reshape(S5_HALVES, gh, HG_C, P_STATE)
        eye = jnp.eye(gh, dtype=part.dtype)
        out.append(jnp.einsum("agop,gk->agpko", p4, eye).reshape(S5_HALVES, gh * P_STATE, gh * HG_C))
    return jnp.concatenate(out, axis=1).astype(BF16)


def _s5_state_in(re, im):
    n = re.shape[0]
    parts = [a.reshape(n, S5_HALVES, 1, S5_HALF_RI) for a in (re, im)]
    return jnp.concatenate(parts, axis=2).reshape(n, S5_LANES)


def _s5_state_out(s):
    n = s.shape[0]
    s4 = s.reshape(n, S5_HALVES, 2, S5_HALF_RI)
    return s4[:, :, 0].reshape(n, G_C, P_STATE), s4[:, :, 1].reshape(n, G_C, P_STATE)


def _lane_vec(v):
    return v.reshape(1, -1)


def _layer_weights(l, w):
    ar, ai, bbr, bbi = _s5_discretize(w["c_lam_re"][l], w["c_lam_im"][l], w["c_log_dt"][l],
                                      w["c_b_re"][l], w["c_b_im"][l])
    causal = jnp.tril(jnp.ones((CHUNK, CHUNK), dtype=bool))
    ws = w["a_ws"][l]
    lw = dict(
        norm_mix_g=_lane_vec(w["norm_mix_g"][l]),
        w_in=w["w_in"][l].astype(BF16),
        b_gate=_lane_vec(w["b_gate"][l]),
        a_ln_g=_lane_vec(w["a_ln_g"][l]),
        a_ln_b=_lane_vec(w["a_ln_b"][l]),
        a_wm=jnp.where(causal[None], ws, jnp.zeros_like(ws)),
        a_ws=ws,
        a_bs=w["a_bs"][l],
        b_cw=w["b_conv_w"][l][:, None, :],
        b_cb=_lane_vec(w["b_conv_b"][l]),
        b_wax=jnp.concatenate([_block_diag(w["b_wa"][l]), _block_diag(w["b_wx"][l])], axis=1).astype(BF16),
        b_ba=_lane_vec(w["b_ba"][l]),
        b_bx=_lane_vec(w["b_bx"][l]),
        b_lam=_lane_vec(w["b_lam"][l]),
        c_ar=ar.reshape(S5_HALVES, S5_HALF_RI).reshape(1, S5_LANES // 2),
        c_ai=ai.reshape(S5_HALVES, S5_HALF_RI).reshape(1, S5_LANES // 2),
        c_bd=_s5_in_matrix(bbr, bbi),
        c_cd=_s5_out_matrix(w["c_c_re"][l], w["c_c_im"][l]),
        c_d=_lane_vec(w["c_d"][l]),
        c_glu_w=w["c_glu_w"][l].astype(BF16),
        c_glu_b=_lane_vec(w["c_glu_b"][l]),
        w_branch=w["w_branch"][l].astype(BF16),
        w_out=w["w_out"][l].astype(BF16),
        norm_ffn_g=_lane_vec(w["norm_ffn_g"][l]),
        final_norm_g=_lane_vec(w["final_norm_g"]),
    )
    j = l // 2
    if l % 2 == 0:
        lw.update(ffn_wg=w["ffn_w_gate"][j].astype(BF16), ffn_wu=w["ffn_w_up"][j].astype(BF16),
                  ffn_wd=w["ffn_w_down"][j].astype(BF16))
    else:
        pad = LANE_TILE - N_EXPERTS
        lw.update(moe_rw=jnp.pad(w["moe_router_w"][j], ((0, 0), (0, pad))),
                  moe_rb=jnp.pad(w["moe_router_b"][j], (0, pad)).reshape(1, LANE_TILE),
                  moe_wg=w["moe_w_gate"][j].astype(BF16), moe_wu=w["moe_w_up"][j].astype(BF16),
                  moe_wd=w["moe_w_down"][j].astype(BF16))
    return lw


def _trunk(x_tm, t, n, conv_s, lru_s, s5re_s, s5im_s, layers):
    x = x_tm
    vas, convs, lrus, sres, sims = [], [], [], [], []
    for l, lw in enumerate(layers):
        proj, gates = _in_proj(x, lw["norm_mix_g"], lw["w_in"], lw["b_gate"], lw["a_ln_g"], lw["a_ln_b"])
        if t % CHUNK == 0:
            bk = jnp.broadcast_to(lw["a_bs"].T[:, None, :, None], (CHUNK, n, H_A, HD_A))
            ya = _mixer_a_prompt(proj, t, n, lw["a_wm"].astype(BF16), bk.reshape(CHUNK * n, D_A))
        else:
            assert t < CHUNK
            wv = jnp.repeat(jnp.transpose(lw["a_ws"][:, :t, :t], (1, 2, 0)), HD_A, axis=-1)[:, :, None, :]
            bv = jnp.repeat(jnp.transpose(lw["a_bs"][:, :t], (1, 0)), HD_A, axis=-1)[:, None, :]
            ya = _mixer_a_sample(proj, t, n, wv, bv)
        yb, conv_new, h_last = _mixer_b(proj, t, n, jnp.swapaxes(conv_s[l], 0, 1), lru_s[l], lw["b_cw"],
                                        lw["b_cb"], lw["b_wax"], lw["b_ba"], lw["b_bx"], lw["b_lam"])
        yc, s_last = _mixer_c(proj, t, n, _s5_state_in(s5re_s[l], s5im_s[l]), lw["c_ar"], lw["c_ai"],
                              lw["c_bd"], lw["c_cd"], lw["c_d"], lw["c_glu_w"], lw["c_glu_b"])
        final = l == len(layers) - 1
        x1 = _merge(ya, yb, yc, gates, x, lw["w_branch"], lw["w_out"])
        if l % 2 == 0:
            x = _dense_ffn(x1, lw["norm_ffn_g"], lw["ffn_wg"], lw["ffn_wu"], lw["ffn_wd"],
                           lw["final_norm_g"], final)
        else:
            x = _moe_ffn(x1, lw["norm_ffn_g"], lw["moe_rw"], lw["moe_rb"], lw["moe_wg"], lw["moe_wu"],
                         lw["moe_wd"], lw["final_norm_g"], final)
        va = proj.reshape(t, n, PA_COLS)[:, :, BLK_VA * COL_BLK:(BLK_VA + 1) * COL_BLK]
        vas.append(jnp.swapaxes(va, 0, 1))
        convs.append(jnp.swapaxes(conv_new, 0, 1))
        lrus.append(h_last)
        s_re, s_im = _s5_state_out(s_last)
        sres.append(s_re)
        sims.append(s_im)
    return x, jnp.stack(vas), jnp.stack(convs), jnp.stack(lrus), jnp.stack(sres), jnp.stack(sims)


def kernel(x_prompt, x_sample, state_conv_b, state_lru_b, state_s5_re, state_s5_im, norm_mix_g, w_in, b_gate, a_ln_g, a_ln_b, a_ws, a_bs, b_conv_w, b_conv_b, b_wa, b_ba, b_wx, b_bx, b_lam, c_lam_re, c_lam_im, c_log_dt, c_b_re, c_b_im, c_c_re, c_c_im, c_d, c_glu_w, c_glu_b, w_branch, w_out, norm_ffn_g, ffn_w_gate, ffn_w_up, ffn_w_down, moe_router_w, moe_router_b, moe_w_gate, moe_w_up, moe_w_down, final_norm_g):
    w = dict(norm_mix_g=norm_mix_g, w_in=w_in, b_gate=b_gate, a_ln_g=a_ln_g, a_ln_b=a_ln_b, a_ws=a_ws,
             a_bs=a_bs, b_conv_w=b_conv_w, b_conv_b=b_conv_b, b_wa=b_wa, b_ba=b_ba, b_wx=b_wx, b_bx=b_bx,
             b_lam=b_lam, c_lam_re=c_lam_re, c_lam_im=c_lam_im, c_log_dt=c_log_dt, c_b_re=c_b_re,
             c_b_im=c_b_im, c_c_re=c_c_re, c_c_im=c_c_im, c_d=c_d, c_glu_w=c_glu_w, c_glu_b=c_glu_b,
             w_branch=w_branch, w_out=w_out, norm_ffn_g=norm_ffn_g, ffn_w_gate=ffn_w_gate,
             ffn_w_up=ffn_w_up, ffn_w_down=ffn_w_down, moe_router_w=moe_router_w,
             moe_router_b=moe_router_b, moe_w_gate=moe_w_gate, moe_w_up=moe_w_up, moe_w_down=moe_w_down,
             final_norm_g=final_norm_g)
    layers = [_layer_weights(l, w) for l in range(DEPTH)]

    nb, tp, _ = x_prompt.shape
    ns, ts, _ = x_sample.shape
    conv0 = jnp.zeros((DEPTH, nb, CONV_W - 1, D_B), F32)
    lru0 = jnp.zeros((DEPTH, nb, D_B), F32)
    s50 = jnp.zeros((DEPTH, nb, G_C, P_STATE), F32)
    xp = jnp.swapaxes(x_prompt, 0, 1).reshape(tp * nb, D_MODEL)
    yp, _, conv_p, lru_p, sre_p, sim_p = _trunk(xp, tp, nb, conv0, lru0, s50, s50, layers)
    xs = jnp.swapaxes(x_sample, 0, 1).reshape(ts * ns, D_MODEL)
    ys, va_s, conv_s, lru_s, sre_s, sim_s = _trunk(xs, ts, ns, state_conv_b, state_lru_b, state_s5_re,
                                                   state_s5_im, layers)
    y_prompt = jnp.swapaxes(yp.reshape(tp, nb, D_MODEL), 0, 1)
    y_sample = jnp.swapaxes(ys.reshape(ts, ns, D_MODEL), 0, 1)
    return (y_prompt, y_sample, conv_p, lru_p, sre_p, sim_p, va_s, conv_s, lru_s, sre_s, sim_s)
```

```python
import functools
import math

import jax
import jax.numpy as jnp
from jax import lax
from jax.experimental import pallas as pl
from jax.experimental.pallas import tpu as pltpu

D_MODEL = 1024
DEPTH = 2
CHUNK = 128
D_A = D_MODEL // 2
H_A = 4
HD_A = D_A // H_A
D_B = D_MODEL // 2
H_B = 8
BH_B = D_B // H_B
CONV_W = 4
LRU_C = 8.0
D_C = D_MODEL // 2
HG_C = 16
G_C = D_C // HG_C
P_STATE = 64
N_BRANCH = 3
D_BR = D_MODEL // 2
IN_COLS = 2 * D_A + 2 * D_B + D_C + N_BRANCH * D_MODEL
D_FF = 11 * D_MODEL // 4
N_EXPERTS = 8
TOP_K = 2
D_EXP = D_FF // 2
EPS = 1e-6

F32 = jnp.float32
BF16 = jnp.bfloat16

COL_BLK = 512
N_COL_BLK = IN_COLS // COL_BLK
BLK_UA, BLK_VA, BLK_XB, BLK_GB, BLK_UC, BLK_GATE = 0, 1, 2, 3, 4, 5
PA_COLS = BLK_GATE * COL_BLK
S5_HALVES = 2
S5_HALF_IN = D_C // S5_HALVES
S5_HALF_RI = (G_C // S5_HALVES) * P_STATE
S5_HALF_ST = 2 * S5_HALF_RI
S5_LANES = S5_HALVES * S5_HALF_ST
LANE_TILE = 128
SUBLANE_TILE = 8
MXU_DIM = 256
FFN_CHUNK = 3 * MXU_DIM
VMEM_LIMIT = 56 * 1024 * 1024


def _resident(shape):
    zeros = (0,) * len(shape)
    return pl.BlockSpec(shape, lambda *_: zeros)


def _params(sem):
    return pltpu.CompilerParams(dimension_semantics=sem, vmem_limit_bytes=VMEM_LIMIT)


def _rms(x, g):
    return x * lax.rsqrt(jnp.mean(x * x, axis=-1, keepdims=True) + EPS) * g


def _bdot(a, b):
    return jnp.dot(a.astype(BF16), b, preferred_element_type=F32)


def _in_proj_kernel(x_ref, g_ref, w_ref, bg_ref, lng_ref, lnb_ref, pa_ref, gate_ref):
    half = x_ref.shape[0] // 2
    for r0 in (0, half):
        rows = slice(r0, r0 + half)
        hb = _rms(x_ref[rows, :], g_ref[...]).astype(BF16)
        for j in range(N_COL_BLK):
            cols = slice(j * COL_BLK, (j + 1) * COL_BLK)
            acc = jnp.dot(hb, w_ref[:, cols], preferred_element_type=F32)
            if j in (BLK_UA, BLK_GB):
                pa_ref[rows, cols] = jax.nn.gelu(acc)
            elif j == BLK_VA:
                v = jax.nn.gelu(acc)
                vc = v - jnp.mean(v, axis=-1, keepdims=True)
                var = jnp.mean(vc * vc, axis=-1, keepdims=True)
                pa_ref[rows, cols] = vc * lax.rsqrt(var + EPS) * lng_ref[...] + lnb_ref[...]
            elif j in (BLK_XB, BLK_UC):
                pa_ref[rows, cols] = acc
            else:
                gcols = slice(cols.start - PA_COLS, cols.stop - PA_COLS)
                gate_ref[rows, gcols] = jax.nn.sigmoid(acc + bg_ref[:, gcols]).astype(BF16)


def _in_proj(x, norm_g, w_in, b_gate, ln_g, ln_b):
    r = x.shape[0]
    tm = min(r, 512)
    return pl.pallas_call(
        _in_proj_kernel,
        grid=(r // tm,),
        in_specs=[
            pl.BlockSpec((tm, D_MODEL), lambda i: (i, 0)),
            _resident((1, D_MODEL)),
            _single((D_MODEL, IN_COLS)),
            _resident((1, N_BRANCH * D_MODEL)),
            _resident((1, D_A)),
            _resident((1, D_A)),
        ],
        out_specs=[
            pl.BlockSpec((tm, PA_COLS), lambda i: (i, 0)),
            pl.BlockSpec((tm, N_BRANCH * D_MODEL), lambda i: (i, 0)),
        ],
        out_shape=[
            jax.ShapeDtypeStruct((r, PA_COLS), F32),
            jax.ShapeDtypeStruct((r, N_BRANCH * D_MODEL), BF16),
        ],
        compiler_params=_params(("parallel",)),
        name="in_proj",
    )(x, norm_g, w_in, b_gate, ln_g, ln_b)


def _mixer_a_mxu_kernel(u_ref, v_ref, wm_ref, bk_ref, o_ref, wk_scr, *, rows, n):
    @pl.when(pl.program_id(0) == 0)
    def _():
        shift = n.bit_length() - 1
        e_rows = (lax.broadcasted_iota(jnp.int32, (rows, CHUNK), 0) >> shift
                  == lax.broadcasted_iota(jnp.int32, (rows, CHUNK), 1)).astype(BF16)
        e_cols = (lax.broadcasted_iota(jnp.int32, (CHUNK, rows), 0)
                  == lax.broadcasted_iota(jnp.int32, (CHUNK, rows), 1) >> shift).astype(BF16)
        same_seq = ((lax.broadcasted_iota(jnp.int32, (rows, rows), 0) & (n - 1))
                    == (lax.broadcasted_iota(jnp.int32, (rows, rows), 1) & (n - 1)))
        for h in range(H_A):
            left = jnp.dot(e_rows, wm_ref[h], preferred_element_type=F32).astype(BF16)
            full = jnp.dot(left, e_cols, preferred_element_type=F32)
            wk_scr[h] = jnp.where(same_seq, full, 0.0).astype(BF16)

    vb = v_ref[...].astype(BF16)
    for h in range(H_A):
        cols = slice(h * HD_A, (h + 1) * HD_A)
        for r0 in range(0, rows, MXU_DIM):
            k = r0 + MXU_DIM
            mixed = jnp.dot(wk_scr[h, r0:k, 0:k], vb[0:k, cols], preferred_element_type=F32)
            o_ref[r0:k, cols] = (u_ref[r0:k, cols] * (mixed + bk_ref[r0:k, cols])).astype(BF16)


def _mixer_a_prompt(pa, t, n, wm, bk):
    rows = CHUNK * n
    assert rows % MXU_DIM == 0 and MXU_DIM % n == 0 and n & (n - 1) == 0
    return pl.pallas_call(
        functools.partial(_mixer_a_mxu_kernel, rows=rows, n=n),
        grid=(t // CHUNK,),
        in_specs=[
            pl.BlockSpec((rows, COL_BLK), lambda c: (c, BLK_UA)),
            pl.BlockSpec((rows, COL_BLK), lambda c: (c, BLK_VA)),
            _single((H_A, CHUNK, CHUNK)),
            _single((rows, D_A)),
        ],
        out_specs=pl.BlockSpec((rows, D_A), lambda c: (c, 0)),
        out_shape=jax.ShapeDtypeStruct((t * n, D_A), BF16),
        scratch_shapes=[pltpu.VMEM((H_A, rows, rows), BF16)],
        compiler_params=_params(("arbitrary",)),
        name="mixer_a_prompt",
    )(pa, pa, wm, bk)


def _mixer_a_vpu_kernel(u_ref, v_ref, wv_ref, bv_ref, o_ref, *, t, n):
    for i in range(t):
        mixed = wv_ref[i, 0] * v_ref[0]
        for s in range(1, i + 1):
            mixed = mixed + wv_ref[i, s] * v_ref[s]
        o_ref[i * n:(i + 1) * n] = (u_ref[i] * (mixed + bv_ref[i])).astype(BF16)


def _mixer_a_sample(proj, t, n, wv, bv):
    proj3 = proj.reshape(t, n, PA_COLS)
    return pl.pallas_call(
        functools.partial(_mixer_a_vpu_kernel, t=t, n=n),
        grid=(1,),
        in_specs=[
            pl.BlockSpec((t, n, COL_BLK), lambda i: (0, 0, BLK_UA)),
            pl.BlockSpec((t, n, COL_BLK), lambda i: (0, 0, BLK_VA)),
            _resident((t, t, 1, D_A)),
            _resident((t, 1, D_A)),
        ],
        out_specs=_resident((t * n, D_A)),
        out_shape=jax.ShapeDtypeStruct((t * n, D_A), BF16),
        compiler_params=_params(("arbitrary",)),
        name="mixer_a_sample",
    )(proj3, proj3, wv, bv)


def _lru_kernel(xb_ref, gb_ref, conv0_ref, h0_ref, cw_ref, cb_ref, wax_ref, ba_ref, bx_ref, lam_ref,
                yb_ref, convn_ref, hlast_ref, xp_scr, a_scr, d_scr, h_scr, *, tt, n):
    @pl.when(pl.program_id(0) == 0)
    def _():
        xp_scr[0:CONV_W - 1] = conv0_ref[...]
        h_scr[...] = h0_ref[...]

    xp_scr[CONV_W - 1:CONV_W - 1 + tt] = xb_ref[...]
    xc = cb_ref[...] + xp_scr[0:tt] * cw_ref[0]
    for k in range(1, CONV_W):
        xc = xc + xp_scr[k:k + tt] * cw_ref[k]
    tail = xp_scr[tt:tt + CONV_W - 1]
    convn_ref[...] = tail
    xp_scr[0:CONV_W - 1] = tail

    xc2 = xc.reshape(tt * n, D_B)
    pre = _bdot(xc2, wax_ref[...])
    r = jax.nn.sigmoid(pre[:, :D_B] + ba_ref[...])
    i = jax.nn.sigmoid(pre[:, D_B:] + bx_ref[...])
    log_a = -LRU_C * r * jax.nn.softplus(-lam_ref[...])
    a = jnp.exp(log_a)
    a_scr[...] = a.reshape(tt, n, D_B)
    gain = jnp.sqrt(-jnp.tanh(log_a) * (a * a + 1.0))
    d_scr[...] = (gain * (i * xc2)).reshape(tt, n, D_B)

    def step(s, h):
        h = a_scr[s] * h + d_scr[s]
        d_scr[s] = h
        return h

    h = h_scr[...]
    if n == SUBLANE_TILE:
        for s in range(tt):
            h = step(s, h)
    else:
        h = lax.fori_loop(0, tt, step, h, unroll=min(tt, 8))
    h_scr[...] = h
    hlast_ref[...] = h
    yb_ref[...] = (d_scr[...] * gb_ref[...]).reshape(tt * n, D_B).astype(BF16)


def _mixer_b(proj, t, n, conv0, h0, cw, cb, wax, ba, bx, lam):
    tt = min(t, 512 // n) if n <= 512 else 1
    proj3 = proj.reshape(t, n, PA_COLS)
    yb, conv_new, h_last = pl.pallas_call(
        functools.partial(_lru_kernel, tt=tt, n=n),
        grid=(t // tt,),
        in_specs=[
            pl.BlockSpec((tt, n, COL_BLK), lambda i: (i, 0, BLK_XB)),
            pl.BlockSpec((tt, n, COL_BLK), lambda i: (i, 0, BLK_GB)),
            _resident((CONV_W - 1, n, D_B)),
            _resident((n, D_B)),
            _resident((CONV_W, 1, D_B)),
            _resident((1, D_B)),
            _resident((D_B, 2 * D_B)),
            _resident((1, D_B)),
            _resident((1, D_B)),
            _resident((1, D_B)),
        ],
        out_specs=[
            pl.BlockSpec((tt * n, D_B), lambda i: (i, 0)),
            _resident((CONV_W - 1, n, D_B)),
            _resident((n, D_B)),
        ],
        out_shape=[
            jax.ShapeDtypeStruct((t * n, D_B), BF16),
            jax.ShapeDtypeStruct((CONV_W - 1, n, D_B), F32),
            jax.ShapeDtypeStruct((n, D_B), F32),
        ],
        scratch_shapes=[
            pltpu.VMEM((tt + CONV_W - 1, n, D_B), F32),
            pltpu.VMEM((tt, n, D_B), F32),
            pltpu.VMEM((tt, n, D_B), F32),
            pltpu.VMEM((n, D_B), F32),
        ],
        compiler_params=_params(("arbitrary",)),
        name="mixer_b",
    )(proj3, proj3, conv0, h0, cw, cb, wax, ba, bx, lam)
    return yb, conv_new, h_last


def _s5_disc_kernel(lr_ref, li_ref, ldt_ref, bre_ref, bim_ref, ar_ref, ai_ref, bbr_ref, bbi_ref):
    lr = lr_ref[...]
    li = li_ref[...]
    dt = jnp.exp(ldt_ref[...])
    mag = jnp.exp(lr * dt)
    ar = mag * jnp.cos(li * dt)
    ai = mag * jnp.sin(li * dt)
    ar_ref[...] = ar
    ai_ref[...] = ai
    den = lr * lr + li * li
    qr = ((ar - 1.0) * lr + ai * li) / den
    qi = (ai * lr - (ar - 1.0) * li) / den
    bre = bre_ref[...]
    bim = bim_ref[...]
    bbr_ref[...] = qr * bre - qi * bim
    bbi_ref[...] = qr * bim + qi * bre


def _s5_discretize(lam_re, lam_im, log_dt, b_re, b_im):
    g3 = jax.ShapeDtypeStruct((G_C, 1, P_STATE), F32)
    b3 = jax.ShapeDtypeStruct((G_C, HG_C, P_STATE), F32)
    return pl.pallas_call(
        _s5_disc_kernel,
        out_shape=[g3, g3, b3, b3],
        name="s5_discretize",
    )(lam_re.reshape(G_C, 1, P_STATE), lam_im.reshape(G_C, 1, P_STATE), log_dt.reshape(G_C, 1, 1),
      jnp.swapaxes(b_re, 1, 2), jnp.swapaxes(b_im, 1, 2))


def _s5_kernel(u_ref, s0_ref, ar_ref, ai_ref, bd_ref, cd_ref, d_ref, gw_ref, gb_ref,
               yc_ref, slast_ref, bu_scr, st_scr, *, tt, n):
    @pl.when(pl.program_id(0) == 0)
    def _():
        st_scr[...] = s0_ref[...]

    u2 = u_ref[...].reshape(tt * n, D_C)
    ub = u2.astype(BF16)
    for hf in range(S5_HALVES):
        bu = jnp.dot(ub[:, hf * S5_HALF_IN:(hf + 1) * S5_HALF_IN], bd_ref[hf], preferred_element_type=F32)
        bu_scr[:, :, hf * S5_HALF_ST:(hf + 1) * S5_HALF_ST] = bu.reshape(tt, n, S5_HALF_ST)

    lane_blk = 4 * LANE_TILE
    for hf in range(S5_HALVES):
        for q in range(S5_HALF_RI // lane_blk):
            lr = hf * S5_HALF_ST + q * lane_blk
            li = lr + S5_HALF_RI
            la = hf * S5_HALF_RI + q * lane_blk
            ar = jnp.broadcast_to(ar_ref[:, la:la + lane_blk], (SUBLANE_TILE, lane_blk))
            ai = jnp.broadcast_to(ai_ref[:, la:la + lane_blk], (SUBLANE_TILE, lane_blk))

            def rows_body(rg, carry, lr=lr, li=li, ar=ar, ai=ai):
                rs = pl.ds(pl.multiple_of(rg * SUBLANE_TILE, SUBLANE_TILE), SUBLANE_TILE)

                def step(s, x):
                    xr, xi = x
                    nr = ar * xr - ai * xi + bu_scr[s, rs, lr:lr + lane_blk]
                    ni = ar * xi + ai * xr + bu_scr[s, rs, li:li + lane_blk]
                    bu_scr[s, rs, lr:lr + lane_blk] = nr
                    bu_scr[s, rs, li:li + lane_blk] = ni
                    return nr, ni

                x0 = (st_scr[rs, lr:lr + lane_blk], st_scr[rs, li:li + lane_blk])
                xr, xi = lax.fori_loop(0, tt, step, x0, unroll=min(tt, 8))
                st_scr[rs, lr:lr + lane_blk] = xr
                st_scr[rs, li:li + lane_blk] = xi
                return carry

            if n == SUBLANE_TILE:
                xr, xi = st_scr[:, lr:lr + lane_blk], st_scr[:, li:li + lane_blk]
                for s in range(tt):
                    nr = ar * xr - ai * xi + bu_scr[s, :, lr:lr + lane_blk]
                    ni = ar * xi + ai * xr + bu_scr[s, :, li:li + lane_blk]
                    bu_scr[s, :, lr:lr + lane_blk] = nr
                    bu_scr[s, :, li:li + lane_blk] = ni
                    xr, xi = nr, ni
                st_scr[:, lr:lr + lane_blk] = xr
                st_scr[:, li:li + lane_blk] = xi
            else:
                lax.fori_loop(0, n // SUBLANE_TILE, rows_body, 0)

    slast_ref[...] = st_scr[...]
    ys = []
    for hf in range(S5_HALVES):
        xs = bu_scr[:, :, hf * S5_HALF_ST:(hf + 1) * S5_HALF_ST].reshape(tt * n, S5_HALF_ST)
        ys.append(_bdot(xs, cd_ref[hf]))
    y = jnp.concatenate(ys, axis=-1) + d_ref[...] * u2
    z = jax.nn.gelu(y)
    yc = z * jax.nn.sigmoid(_bdot(z, gw_ref[...]) + gb_ref[...])
    yc_ref[...] = yc.astype(BF16)


def _mixer_c(proj, t, n, s0, ar, ai, bd, cd, d, glu_w, glu_b):
    tt = min(t, 512 // n) if n <= 512 else 1
    proj3 = proj.reshape(t, n, PA_COLS)
    yc, s_last = pl.pallas_call(
        functools.partial(_s5_kernel, tt=tt, n=n),
        grid=(t // tt,),
        in_specs=[
            pl.BlockSpec((tt, n, COL_BLK), lambda i: (i, 0, BLK_UC)),
            _resident((n, S5_LANES)),
            _resident((1, S5_LANES // 2)),
            _resident((1, S5_LANES // 2)),
            _resident((S5_HALVES, S5_HALF_IN, S5_HALF_ST)),
            _resident((S5_HALVES, S5_HALF_ST, S5_HALF_IN)),
            _resident((1, D_C)),
            _resident((D_C, D_C)),
            _resident((1, D_C)),
        ],
        out_specs=[
            pl.BlockSpec((tt * n, D_C), lambda i: (i, 0)),
            _resident((n, S5_LANES)),
        ],
        out_shape=[
            jax.ShapeDtypeStruct((t * n, D_C), BF16),
            jax.ShapeDtypeStruct((n, S5_LANES), F32),
        ],
        scratch_shapes=[
            pltpu.VMEM((tt, n, S5_LANES), F32),
            pltpu.VMEM((n, S5_LANES), F32),
        ],
        compiler_params=_params(("arbitrary",)),
        name="mixer_c",
    )(proj3, s0, ar, ai, bd, cd, d, glu_w, glu_b)
    return yc, s_last


N_B_IN, N_C_IN, N_B_OUT, N_C_OUT, N_B_SCR = 10, 9, 3, 2, 4


def _mixer_bc_kernel(*refs, tt, n):
    b_in, rest = refs[:N_B_IN], refs[N_B_IN:]
    c_in, rest = rest[:N_C_IN], rest[N_C_IN:]
    b_out, rest = rest[:N_B_OUT], rest[N_B_OUT:]
    c_out, rest = rest[:N_C_OUT], rest[N_C_OUT:]
    b_scr, c_scr = rest[:N_B_SCR], rest[N_B_SCR:]
    _lru_kernel(*b_in, *b_out, *b_scr, tt=tt, n=n)
    _s5_kernel(*c_in, *c_out, *c_scr, tt=tt, n=n)


def _mixer_bc(proj, t, n, conv0, h0, cw, cb, wax, ba, bx, lam, s0, ar, ai, bd, cd, d, glu_w, glu_b):
    tt = min(t, 512 // n) if n <= 512 else 1
    proj3 = proj.reshape(t, n, PA_COLS)
    rows = pl.BlockSpec((tt * n, D_B), lambda i: (i, 0))
    return pl.pallas_call(
        functools.partial(_mixer_bc_kernel, tt=tt, n=n),
        grid=(t // tt,),
        in_specs=[
            pl.BlockSpec((tt, n, COL_BLK), lambda i: (i, 0, BLK_XB)),
            pl.BlockSpec((tt, n, COL_BLK), lambda i: (i, 0, BLK_GB)),
            _resident((CONV_W - 1, n, D_B)),
            _resident((n, D_B)),
            _resident((CONV_W, 1, D_B)),
            _resident((1, D_B)),
            _resident((D_B, 2 * D_B)),
            _resident((1, D_B)),
            _resident((1, D_B)),
            _resident((1, D_B)),
            pl.BlockSpec((tt, n, COL_BLK), lambda i: (i, 0, BLK_UC)),
            _resident((n, S5_LANES)),
            _resident((1, S5_LANES // 2)),
            _resident((1, S5_LANES // 2)),
            _resident((S5_HALVES, S5_HALF_IN, S5_HALF_ST)),
            _resident((S5_HALVES, S5_HALF_ST, S5_HALF_IN)),
            _resident((1, D_C)),
            _resident((D_C, D_C)),
            _resident((1, D_C)),
        ],
        out_specs=[
            rows,
            _resident((CONV_W - 1, n, D_B)),
            _resident((n, D_B)),
            rows,
            _resident((n, S5_LANES)),
        ],
        out_shape=[
            jax.ShapeDtypeStruct((t * n, D_B), BF16),
            jax.ShapeDtypeStruct((CONV_W - 1, n, D_B), F32),
            jax.ShapeDtypeStruct((n, D_B), F32),
            jax.ShapeDtypeStruct((t * n, D_C), BF16),
            jax.ShapeDtypeStruct((n, S5_LANES), F32),
        ],
        scratch_shapes=[
            pltpu.VMEM((tt + CONV_W - 1, n, D_B), F32),
            pltpu.VMEM((tt, n, D_B), F32),
            pltpu.VMEM((tt, n, D_B), F32),
            pltpu.VMEM((n, D_B), F32),
            pltpu.VMEM((tt, n, S5_LANES), F32),
            pltpu.VMEM((n, S5_LANES), F32),
        ],
        compiler_params=_params(("arbitrary",)),
        name="mixer_bc",
    )(proj3, proj3, conv0, h0, cw, cb, wax, ba, bx, lam, proj3, s0, ar, ai, bd, cd, d, glu_w, glu_b)


def _single(shape):
    zeros = (0,) * len(shape)
    return pl.BlockSpec(shape, lambda *_: zeros, pipeline_mode=pl.Buffered(1))


def _merge_kernel(ya_ref, yb_ref, yc_ref, g_ref, x_ref, wbr_ref, wout_ref, o_ref):
    m = None
    for k, y_ref in enumerate((ya_ref, yb_ref, yc_ref)):
        p = _bdot(y_ref[...], wbr_ref[k])
        g = g_ref[:, k * D_MODEL:(k + 1) * D_MODEL].astype(F32)
        m = g * p if m is None else m + g * p
    o_ref[...] = x_ref[...] + _bdot(m, wout_ref[...])


def _merge(ya, yb, yc, gates, x, wbr, wout):
    r = x.shape[0]
    tm = min(r, 512)
    row = pl.BlockSpec((tm, D_MODEL), lambda i: (i, 0))
    return pl.pallas_call(
        _merge_kernel,
        grid=(r // tm,),
        in_specs=[pl.BlockSpec((tm, D_BR), lambda i: (i, 0))] * N_BRANCH + [
            pl.BlockSpec((tm, N_BRANCH * D_MODEL), lambda i: (i, 0)),
            row,
            _single((N_BRANCH, D_BR, D_MODEL)),
            _single((D_MODEL, D_MODEL)),
        ],
        out_specs=row,
        out_shape=jax.ShapeDtypeStruct((r, D_MODEL), F32),
        compiler_params=_params(("parallel",)),
        name="merge",
    )(ya, yb, yc, gates, x, wbr, wout)


def _swiglu(hb, wg_ref, wu_ref, wd_ref):
    hidden = wg_ref.shape[-1]
    out = None
    for c0 in range(0, hidden, FFN_CHUNK):
        cols = slice(c0, min(c0 + FFN_CHUNK, hidden))
        gate = jnp.dot(hb, wg_ref[:, cols], preferred_element_type=F32)
        up = jnp.dot(hb, wu_ref[:, cols], preferred_element_type=F32)
        part = _bdot(jax.nn.silu(gate) * up, wd_ref[cols, :])
        out = part if out is None else out + part
    return out


def _dense_ffn_kernel(x_ref, nf_ref, wg_ref, wu_ref, wd_ref, fin_ref, o_ref, *, final):
    x1 = x_ref[...]
    hb = _rms(x1, nf_ref[...]).astype(BF16)
    x2 = x1 + _swiglu(hb, wg_ref, wu_ref, wd_ref)
    o_ref[...] = _rms(x2, fin_ref[...]) if final else x2


def _dense_ffn(x1, nf, wg, wu, wd, fin, final):
    r = x1.shape[0]
    tm = min(r, 1024)
    row = pl.BlockSpec((tm, D_MODEL), lambda i: (i, 0))
    return pl.pallas_call(
        functools.partial(_dense_ffn_kernel, final=final),
        grid=(r // tm,),
        in_specs=[
            row,
            _resident((1, D_MODEL)),
            _single((D_MODEL, D_FF)),
            _single((D_MODEL, D_FF)),
            _single((D_FF, D_MODEL)),
            _resident((1, D_MODEL)),
        ],
        out_specs=row,
        out_shape=jax.ShapeDtypeStruct((r, D_MODEL), F32),
        compiler_params=_params(("parallel",)),
        name="dense_ffn",
    )(x1, nf, wg, wu, wd, fin)


def _moe_kernel(x_ref, nf_ref, rw_ref, rb_ref, wg_ref, wu_ref, wd_ref, fin_ref, o_ref,
                hb_scr, comb_scr, acc_scr, *, final):
    e = pl.program_id(1)
    lane = lax.broadcasted_iota(jnp.int32, comb_scr.shape, 1)

    @pl.when(e == 0)
    def _():
        hn = _rms(x_ref[...], nf_ref[...])
        hb_scr[...] = hn.astype(BF16)
        hn_hi = hn.astype(BF16)
        hn_lo = (hn - hn_hi.astype(F32)).astype(BF16)
        rw = rw_ref[...]
        rw_hi = rw.astype(BF16)
        rw_lo = (rw - rw_hi.astype(F32)).astype(BF16)
        logits = (jnp.dot(hn_hi, rw_hi, preferred_element_type=F32)
                  + (jnp.dot(hn_lo, rw_hi, preferred_element_type=F32)
                     + jnp.dot(hn_hi, rw_lo, preferred_element_type=F32))) + rb_ref[...]
        neg = jnp.float32(-jnp.inf)
        lg = jnp.where(lane < N_EXPERTS, logits, neg)
        m1 = jnp.max(lg, axis=-1, keepdims=True)
        i1 = jnp.min(jnp.where(lg == m1, lane, LANE_TILE), axis=-1, keepdims=True)
        lg2 = jnp.where(lane == i1, neg, lg)
        m2 = jnp.max(lg2, axis=-1, keepdims=True)
        i2 = jnp.min(jnp.where(lg2 == m2, lane, LANE_TILE), axis=-1, keepdims=True)
        e2 = jnp.exp(m2 - m1)
        den = 1.0 + e2
        comb_scr[...] = jnp.where(lane == i1, 1.0 / den, 0.0) + jnp.where(lane == i2, e2 / den, 0.0)
        acc_scr[...] = jnp.zeros_like(acc_scr)

    out_e = _swiglu(hb_scr[...], wg_ref, wu_ref, wd_ref)
    w_e = jnp.sum(jnp.where(lane == e, comb_scr[...], 0.0), axis=-1, keepdims=True)
    acc_scr[...] += w_e * out_e

    @pl.when(e == N_EXPERTS - 1)
    def _():
        x2 = x_ref[...] + acc_scr[...]
        o_ref[...] = _rms(x2, fin_ref[...]) if final else x2


def _moe_ffn(x1, nf, rw, rb, wg, wu, wd, fin, final):
    r = x1.shape[0]
    tm = min(r, 1024)
    row = pl.BlockSpec((tm, D_MODEL), lambda i, e: (i, 0))
    return pl.pallas_call(
        functools.partial(_moe_kernel, final=final),
        grid=(r // tm, N_EXPERTS),
        in_specs=[
            row,
            _resident((1, D_MODEL)),
            _single((D_MODEL, LANE_TILE)),
            _resident((1, LANE_TILE)),
            pl.BlockSpec((None, D_MODEL, D_EXP), lambda i, e: (e, 0, 0)),
            pl.BlockSpec((None, D_MODEL, D_EXP), lambda i, e: (e, 0, 0)),
            pl.BlockSpec((None, D_EXP, D_MODEL), lambda i, e: (e, 0, 0)),
            _resident((1, D_MODEL)),
        ],
        out_specs=row,
        out_shape=jax.ShapeDtypeStruct((r, D_MODEL), F32),
        scratch_shapes=[
            pltpu.VMEM((tm, D_MODEL), BF16),
            pltpu.VMEM((tm, LANE_TILE), F32),
            pltpu.VMEM((tm, D_MODEL), F32),
        ],
        compiler_params=_params(("parallel", "arbitrary")),
        name="moe_ffn",
    )(x1, nf, rw, rb, wg, wu, wd, fin)


def _block_diag(w):
    h, i, j = w.shape
    eye = jnp.eye(h, dtype=w.dtype)
    return jnp.einsum("hij,hk->hikj", w, eye).reshape(h * i, h * j)


def _s5_in_matrix(bbr, bbi):
    gh = G_C // S5_HALVES
    out = []
    for part in (bbr, bbi):
        p4 = part.reshape(S5_HALVES, gh, HG_C, P_STATE)
        eye = jnp.eye(gh, dtype=part.dtype)
        out.append(jnp.einsum("aghp,gk->aghkp", p4, eye).reshape(S5_HALVES, gh * HG_C, gh * P_STATE))
    return jnp.concatenate(out, axis=-1).astype(BF16)


def _s5_out_matrix(c_re, c_im):
    gh = G_C // S5_HALVES
    out = []
    for part in (c_re, -c_im):
        p4 = part.reshape(S5_HALVES, gh, HG_C, P_STATE)
        eye = jnp.eye(gh, dtype=part.dtype)
        out.append(jnp.einsum("agop,gk->agpko", p4, eye).reshape(S5_HALVES, gh * P_STATE, gh * HG_C))
    return jnp.concatenate(out, axis=1).astype(BF16)


def _s5_state_in(re, im):
    n = re.shape[0]
    parts = [a.reshape(n, S5_HALVES, 1, S5_HALF_RI) for a in (re, im)]
    return jnp.concatenate(parts, axis=2).reshape(n, S5_LANES)


def _s5_state_out(s):
    n = s.shape[0]
    s4 = s.reshape(n, S5_HALVES, 2, S5_HALF_RI)
    return s4[:, :, 0].reshape(n, G_C, P_STATE), s4[:, :, 1].reshape(n, G_C, P_STATE)


def _lane_vec(v):
    return v.reshape(1, -1)


def _layer_weights(l, w):
    ar, ai, bbr, bbi = _s5_discretize(w["c_lam_re"][l], w["c_lam_im"][l], w["c_log_dt"][l],
                                      w["c_b_re"][l], w["c_b_im"][l])
    causal = jnp.tril(jnp.ones((CHUNK, CHUNK), dtype=bool))
    ws = w["a_ws"][l]
    lw = dict(
        norm_mix_g=_lane_vec(w["norm_mix_g"][l]),
        w_in=w["w_in"][l].astype(BF16),
        b_gate=_lane_vec(w["b_gate"][l]),
        a_ln_g=_lane_vec(w["a_ln_g"][l]),
        a_ln_b=_lane_vec(w["a_ln_b"][l]),
        a_wm=jnp.where(causal[None], ws, jnp.zeros_like(ws)),
        a_ws=ws,
        a_bs=w["a_bs"][l],
        b_cw=w["b_conv_w"][l][:, None, :],
        b_cb=_lane_vec(w["b_conv_b"][l]),
        b_wax=jnp.concatenate([_block_diag(w["b_wa"][l]), _block_diag(w["b_wx"][l])], axis=1).astype(BF16),
        b_ba=_lane_vec(w["b_ba"][l]),
        b_bx=_lane_vec(w["b_bx"][l]),
        b_lam=_lane_vec(w["b_lam"][l]),
        c_ar=ar.reshape(S5_HALVES, S5_HALF_RI).reshape(1, S5_LANES // 2),
        c_ai=ai.reshape(S5_HALVES, S5_HALF_RI).reshape(1, S5_LANES // 2),
        c_bd=_s5_in_matrix(bbr, bbi),
        c_cd=_s5_out_matrix(w["c_c_re"][l], w["c_c_im"][l]),
        c_d=_lane_vec(w["c_d"][l]),
        c_glu_w=w["c_glu_w"][l].astype(BF16),
        c_glu_b=_lane_vec(w["c_glu_b"][l]),
        w_branch=w["w_branch"][l].astype(BF16),
        w_out=w["w_out"][l].astype(BF16),
        norm_ffn_g=_lane_vec(w["norm_ffn_g"][l]),
        final_norm_g=_lane_vec(w["final_norm_g"]),
    )
    j = l // 2
    if l % 2 == 0:
        lw.update(ffn_wg=w["ffn_w_gate"][j].astype(BF16), ffn_wu=w["ffn_w_up"][j].astype(BF16),
                  ffn_wd=w["ffn_w_down"][j].astype(BF16))
    else:
        pad = LANE_TILE - N_EXPERTS
        lw.update(moe_rw=jnp.pad(w["moe_router_w"][j], ((0, 0), (0, pad))),
                  moe_rb=jnp.pad(w["moe_router_b"][j], (0, pad)).reshape(1, LANE_TILE),
                  moe_wg=w["moe_w_gate"][j].astype(BF16), moe_wu=w["moe_w_up"][j].astype(BF16),
                  moe_wd=w["moe_w_down"][j].astype(BF16))
    return lw


def _trunk(x_tm, t, n, conv_s, lru_s, s5re_s, s5im_s, layers):
    x = x_tm
    vas, convs, lrus, sres, sims = [], [], [], [], []
    for l, lw in enumerate(layers):
        proj, gates = _in_proj(x, lw["norm_mix_g"], lw["w_in"], lw["b_gate"], lw["a_ln_g"], lw["a_ln_b"])
        if t % CHUNK == 0:
            bk = jnp.broadcast_to(lw["a_bs"].T[:, None, :, None], (CHUNK, n, H_A, HD_A))
            ya = _mixer_a_prompt(proj, t, n, lw["a_wm"].astype(BF16), bk.reshape(CHUNK * n, D_A))
        else:
            assert t < CHUNK
            wv = jnp.repeat(jnp.transpose(lw["a_ws"][:, :t, :t], (1, 2, 0)), HD_A, axis=-1)[:, :, None, :]
            bv = jnp.repeat(jnp.transpose(lw["a_bs"][:, :t], (1, 0)), HD_A, axis=-1)[:, None, :]
            ya = _mixer_a_sample(proj, t, n, wv, bv)
        yb, conv_new, h_last, yc, s_last = _mixer_bc(
            proj, t, n, jnp.swapaxes(conv_s[l], 0, 1), lru_s[l], lw["b_cw"], lw["b_cb"], lw["b_wax"],
            lw["b_ba"], lw["b_bx"], lw["b_lam"], _s5_state_in(s5re_s[l], s5im_s[l]), lw["c_ar"], lw["c_ai"],
            lw["c_bd"], lw["c_cd"], lw["c_d"], lw["c_glu_w"], lw["c_glu_b"])
        final = l == len(layers) - 1
        x1 = _merge(ya, yb, yc, gates, x, lw["w_branch"], lw["w_out"])
        if l % 2 == 0:
            x = _dense_ffn(x1, lw["norm_ffn_g"], lw["ffn_wg"], lw["ffn_wu"], lw["ffn_wd"],
                           lw["final_norm_g"], final)
        else:
            x = _moe_ffn(x1, lw["norm_ffn_g"], lw["moe_rw"], lw["moe_rb"], lw["moe_wg"], lw["moe_wu"],
                         lw["moe_wd"], lw["final_norm_g"], final)
        va = proj.reshape(t, n, PA_COLS)[:, :, BLK_VA * COL_BLK:(BLK_VA + 1) * COL_BLK]
        vas.append(jnp.swapaxes(va, 0, 1))
        convs.append(jnp.swapaxes(conv_new, 0, 1))
        lrus.append(h_last)
        s_re, s_im = _s5_state_out(s_last)
        sres.append(s_re)
        sims.append(s_im)
    return x, jnp.stack(vas), jnp.stack(convs), jnp.stack(lrus), jnp.stack(sres), jnp.stack(sims)


def kernel(x_prompt, x_sample, state_conv_b, state_lru_b, state_s5_re, state_s5_im, norm_mix_g, w_in, b_gate, a_ln_g, a_ln_b, a_ws, a_bs, b_conv_w, b_conv_b, b_wa, b_ba, b_wx, b_bx, b_lam, c_lam_re, c_lam_im, c_log_dt, c_b_re, c_b_im, c_c_re, c_c_im, c_d, c_glu_w, c_glu_b, w_branch, w_out, norm_ffn_g, ffn_w_gate, ffn_w_up, ffn_w_down, moe_router_w, moe_router_b, moe_w_gate, moe_w_up, moe_w_down, final_norm_g):
    w = dict(norm_mix_g=norm_mix_g, w_in=w_in, b_gate=b_gate, a_ln_g=a_ln_g, a_ln_b=a_ln_b, a_ws=a_ws,
             a_bs=a_bs, b_conv_w=b_conv_w, b_conv_b=b_conv_b, b_wa=b_wa, b_ba=b_ba, b_wx=b_wx, b_bx=b_bx,
             b_lam=b_lam, c_lam_re=c_lam_re, c_lam_im=c_lam_im, c_log_dt=c_log_dt, c_b_re=c_b_re,
             c_b_im=c_b_im, c_c_re=c_c_re, c_c_im=c_c_im, c_d=c_d, c_glu_w=c_glu_w, c_glu_b=c_glu_b,
             w_branch=w_branch, w_out=w_out, norm_ffn_g=norm_ffn_g, ffn_w_gate=ffn_w_gate,
             ffn_w_up=ffn_w_up, ffn_w_down=ffn_w_down, moe_router_w=moe_router_w,
             moe_router_b=moe_router_b, moe_w_gate=moe_w_gate, moe_w_up=moe_w_up, moe_w_down=moe_w_down,
             final_norm_g=final_norm_g)
    layers = [_layer_weights(l, w) for l in range(DEPTH)]

    nb, tp, _ = x_prompt.shape
    ns, ts, _ = x_sample.shape
    conv0 = jnp.zeros((DEPTH, nb, CONV_W - 1, D_B), F32)
    lru0 = jnp.zeros((DEPTH, nb, D_B), F32)
    s50 = jnp.zeros((DEPTH, nb, G_C, P_STATE), F32)
    xp = jnp.swapaxes(x_prompt, 0, 1).reshape(tp * nb, D_MODEL)
    yp, _, conv_p, lru_p, sre_p, sim_p = _trunk(xp, tp, nb, conv0, lru0, s50, s50, layers)
    xs = jnp.swapaxes(x_sample, 0, 1).reshape(ts * ns, D_MODEL)
    ys, va_s, conv_s, lru_s, sre_s, sim_s = _trunk(xs, ts, ns, state_conv_b, state_lru_b, state_s5_re,
                                                   state_s5_im, layers)
    y_prompt = jnp.swapaxes(yp.reshape(tp, nb, D_MODEL), 0, 1)
    y_sample = jnp.swapaxes(ys.reshape(ts, ns, D_MODEL), 0, 1)
    return (y_prompt, y_sample, conv_p, lru_p, sre_p, sim_p, va_s, conv_s, lru_s, sre_s, sim_s)
```
